```python
import jax, jax.numpy as jnp
from jax import lax
import numpy as np

D_MODEL = 2048
BATCH = 16
SEQ = 2048
DEPTH = 4

CTX_LEN = 256
GRID_W = 64
N_MIXERS = 4
EPS = 1e-6
NEG = -1e30
CHUNK = 64
ROPE_THETA = 10000.0
RET_HEADS = 8
RET_DK = D_MODEL // RET_HEADS
RET_DV = D_MODEL // RET_HEADS
ML_HEADS = 8
ML_DK = D_MODEL // (2 * ML_HEADS)
ML_DV = D_MODEL // ML_HEADS
NA_HEADS = 16
NA_DH = D_MODEL // NA_HEADS
NA_WIN_R = 8
NA_WIN_C = 16
NA_QB = 8
NA_KB = NA_QB + NA_WIN_C
HG_HEADS = 16
HG_DK = D_MODEL // HG_HEADS
HG_DV = D_MODEL // HG_HEADS
N_EXPERTS = 16
EXPERT_FF = 1024
EC_CAPACITY = 2

kernel_name = "hybrid_ret_mlstm_na_hgrn2_ecmoe_dit"


def rmsnorm(x, g):
    xf = x.astype(jnp.float32)
    y = xf * lax.rsqrt(jnp.mean(xf * xf, -1, keepdims=True) + EPS)
    return (y * g.astype(jnp.float32)).astype(x.dtype)


def modulate(h, shift, scale):
    return h * (1.0 + scale) + shift


def head_norm(o, w, b=None, center=False):
    B, T = o.shape[:2]
    o = o.astype(jnp.float32)
    if center:
        o = o - jnp.mean(o, -1, keepdims=True)
    o = o * lax.rsqrt(jnp.mean(o * o, -1, keepdims=True) + EPS)
    o = o.reshape(B, T, -1) * w.astype(jnp.float32)
    if b is not None:
        o = o + b.astype(jnp.float32)
    return o


def axial_rope(x):
    T, dh = x.shape[1], x.shape[-1]
    t = jnp.arange(T)
    row = (t // GRID_W).astype(jnp.float32)
    col = (t % GRID_W).astype(jnp.float32)
    quarter = dh // 4
    inv = ROPE_THETA ** (-jnp.arange(quarter, dtype=jnp.float32) / quarter)
    ang = jnp.concatenate([row[:, None] * inv, col[:, None] * inv], -1)
    cos = jnp.cos(ang)[None, :, None, :]
    sin = jnp.sin(ang)[None, :, None, :]
    xf = x.astype(jnp.float32)
    x1, x2 = xf[..., : dh // 2], xf[..., dh // 2:]
    return jnp.concatenate([x1 * cos - x2 * sin, x1 * sin + x2 * cos], -1).astype(x.dtype)


def _to_chunks(a, cs):
    B, T = a.shape[:2]
    a = a.astype(jnp.float32).reshape(B, T // cs, cs, *a.shape[2:])
    return jnp.swapaxes(jnp.moveaxis(a, 1, 0), 2, 3)


def chunk_gla(q, k, v, log_f, state):
    B, T, H, _ = q.shape
    cs = min(CHUNK, T)
    qc, kc, vc, gc = (_to_chunks(a, cs) for a in (q, k, v, log_f))
    b = jnp.cumsum(gc, axis=3)
    b_end = b[:, :, :, -1:, :]
    q_in = qc * jnp.exp(b)
    k_in = kc * jnp.exp(-b)
    k_end = kc * jnp.exp(b_end - b)
    decay_end = jnp.exp(b_end[:, :, :, 0, :])
    lower = jnp.tril(jnp.ones((cs, cs), bool))

    def step(S, xs):
        qi, ki, ke, vi, de = xs
        A = jnp.where(lower, jnp.einsum('bhik,bhjk->bhij', qi, ki), 0.0)
        o = jnp.einsum('bhij,bhjv->bhiv', A, vi) + jnp.einsum('bhik,bhkv->bhiv', qi, S)
        S = de[..., None] * S + jnp.einsum('bhjk,bhjv->bhkv', ke, vi)
        return S, o

    S, o = lax.scan(step, state, (q_in, k_in, k_end, vc, decay_end))
    return o.transpose(1, 0, 3, 2, 4).reshape(B, T, H, -1), S


def chunk_mlstm(q, k, v, log_i, log_f, state):
    B, T, H, _ = q.shape
    cs = min(CHUNK, T)
    qc, kc, vc, ic, fc = (_to_chunks(a, cs) for a in (q, k, v, log_i, log_f))
    bc = jnp.cumsum(fc, axis=-1)
    lower = jnp.tril(jnp.ones((cs, cs), bool))

    def step(carry, xs):
        C, n, m = carry
        qi, ki, vi, ii, bi = xs
        logw = jnp.where(lower, bi[..., :, None] - bi[..., None, :] + ii[..., None, :], -jnp.inf)
        log_carry = bi + m[..., None]
        m_i = jnp.maximum(log_carry, jnp.max(logw, -1))
        s = jnp.einsum('bhid,bhjd->bhij', qi, ki) * jnp.exp(logw - m_i[..., None])
        a = jnp.exp(log_carry - m_i)
        num = jnp.einsum('bhij,bhjv->bhiv', s, vi) + a[..., None] * jnp.einsum('bhid,bhdv->bhiv', qi, C)
        den = jnp.sum(s, -1) + a * jnp.einsum('bhid,bhd->bhi', qi, n)
        h = num / jnp.maximum(jnp.abs(den), jnp.exp(-m_i))[..., None]
        b_end = bi[..., -1]
        logw_end = b_end[..., None] - bi + ii
        m_new = jnp.maximum(b_end + m, jnp.max(logw_end, -1))
        decay = jnp.exp(b_end + m - m_new)
        w_end = jnp.exp(logw_end - m_new[..., None])
        C = decay[..., None, None] * C + jnp.einsum('bhj,bhjd,bhjv->bhdv', w_end, ki, vi)
        n = decay[..., None] * n + jnp.einsum('bhj,bhjd->bhd', w_end, ki)
        return (C, n, m_new), h

    state, h = lax.scan(step, state, (qc, kc, vc, ic, bc))
    return h.transpose(1, 0, 3, 2, 4).reshape(B, T, H, -1), state


def bidirectional(scan, ctx_fwd, lat_fwd, ctx_bwd, lat_bwd, state0):
    rev = lambda args: tuple(a[:, ::-1] for a in args)
    oc_f, sc_f = scan(*ctx_fwd, state0)
    ol_f, _ = scan(*lat_fwd, sc_f)
    oc_b, sc_b = scan(*rev(ctx_bwd), state0)
    ol_b, _ = scan(*rev(lat_bwd), sc_b)
    return oc_f + oc_b[:, ::-1], ol_f + ol_b[:, ::-1]


def retention_mixer(a_ctx, a_lat, wq, wk, wv, wg, wo, decay, gn_w, gn_b):
    log_gamma = jax.nn.log_sigmoid(decay.astype(jnp.float32))

    def proj(a, grid):
        B, T, _ = a.shape
        q = (a @ wq).reshape(B, T, RET_HEADS, RET_DK)
        k = (a @ wk).reshape(B, T, RET_HEADS, RET_DK) * RET_DK ** -0.5
        v = (a @ wv).reshape(B, T, RET_HEADS, RET_DV)
        if grid:
            q, k = axial_rope(q), axial_rope(k)
        lg = [jnp.broadcast_to(log_gamma[d][:, None], (B, T, RET_HEADS, RET_DK)) for d in range(2)]
        return (q, k, v, lg[0]), (q, k, v, lg[1])

    ctx_f, ctx_b = proj(a_ctx, False)
    lat_f, lat_b = proj(a_lat, True)
    s0 = jnp.zeros((a_lat.shape[0], RET_HEADS, RET_DK, RET_DV), jnp.float32)
    o_ctx, o_lat = bidirectional(chunk_gla, ctx_f, lat_f, ctx_b, lat_b, s0)

    def out(a, o):
        o = head_norm(o, gn_w, gn_b, center=True)
        return (jax.nn.silu(a @ wg) * o).astype(a.dtype) @ wo

    return out(a_ctx, o_ctx), out(a_lat, o_lat)


def mlstm_mixer(a_ctx, a_lat, wq, wk, wv, wog, wgate, bgate, norm_w, wout):
    def proj(a):
        B, T, _ = a.shape
        q = (a @ wq).reshape(B, T, ML_HEADS, ML_DK)
        k = (a @ wk).reshape(B, T, ML_HEADS, ML_DK) * ML_DK ** -0.5
        v = (a @ wv).reshape(B, T, ML_HEADS, ML_DV)
        dirs = []
        for d in range(2):
            g = (jnp.einsum('btd,dg->btg', a, wgate[d]) + bgate[d]).astype(jnp.float32)
            dirs.append((q, k, v, g[..., :ML_HEADS], jax.nn.log_sigmoid(g[..., ML_HEADS:])))
        return dirs

    ctx_f, ctx_b = proj(a_ctx)
    lat_f, lat_b = proj(a_lat)
    B = a_lat.shape[0]
    s0 = (jnp.zeros((B, ML_HEADS, ML_DK, ML_DV), jnp.float32),
          jnp.zeros((B, ML_HEADS, ML_DK), jnp.float32),
          jnp.zeros((B, ML_HEADS), jnp.float32))
    o_ctx, o_lat = bidirectional(chunk_mlstm, ctx_f, lat_f, ctx_b, lat_b, s0)

    def out(a, o):
        o = head_norm(o, norm_w, center=True)
        return (jax.nn.sigmoid(a @ wog) * o).astype(a.dtype) @ wout

    return out(a_ctx, o_ctx), out(a_lat, o_lat)


def _na_tables(rows):
    wr = min(NA_WIN_R, rows)
    nb = GRID_W // NA_QB
    qr = np.arange(rows)
    r0 = np.clip(qr - wr // 2, 0, rows - wr)
    blk = np.arange(nb)
    s0 = np.clip(blk * NA_QB - NA_WIN_C // 2, 0, GRID_W - NA_KB)
    krow = r0[:, None] + np.arange(wr)[None, :]
    kcol = s0[:, None] + np.arange(NA_KB)[None, :]
    krow_f = np.broadcast_to(krow[:, None, :, None], (rows, nb, wr, NA_KB)).reshape(rows, nb, -1)
    kcol_f = np.broadcast_to(kcol[None, :, None, :], (rows, nb, wr, NA_KB)).reshape(rows, nb, -1)
    key_idx = krow_f * GRID_W + kcol_f
    qcol = blk[:, None] * NA_QB + np.arange(NA_QB)[None, :]
    c0 = np.clip(qcol - NA_WIN_C // 2, 0, GRID_W - NA_WIN_C)
    kc4 = kcol_f[:, :, None, :]
    col_ok = (kc4 >= c0[None, :, :, None]) & (kc4 < c0[None, :, :, None] + NA_WIN_C)
    dr = krow_f[:, :, None, :] - qr[:, None, None, None]
    dc = np.clip(kc4 - qcol[None, :, :, None], -(NA_WIN_C - 1), NA_WIN_C - 1)
    rel_idx = (dr + NA_WIN_R - 1) * (2 * NA_WIN_C - 1) + (dc + NA_WIN_C - 1)
    return key_idx.astype(np.int32), col_ok, rel_idx.astype(np.int32)


def na_mixer(a_ctx, a_lat, wqkv, rpb, wo):
    B, N, D = a_lat.shape
    L = a_ctx.shape[1]
    rows = N // GRID_W
    nb = GRID_W // NA_QB
    scale = NA_DH ** -0.5

    def heads(a):
        T = a.shape[1]
        qkv = (a @ wqkv).reshape(B, T, 3, NA_HEADS, NA_DH).transpose(2, 0, 3, 1, 4)
        return qkv[0], qkv[1], qkv[2]

    qc, kc, vc = heads(a_ctx)
    ql, kl, vl = heads(a_lat)
    pc = jax.nn.softmax(jnp.einsum('bhqd,bhkd->bhqk', qc, kc).astype(jnp.float32) * scale, -1)
    o_ctx = jnp.einsum('bhqk,bhkd->bhqd', pc.astype(vc.dtype), vc).transpose(0, 2, 1, 3).reshape(B, L, D)
    key_idx, col_ok, rel_idx = _na_tables(rows)
    bias = rpb.reshape(NA_HEADS, -1).astype(jnp.float32)[:, rel_idx]
    bias = jnp.where(jnp.asarray(col_ok)[None], bias, NEG).transpose(1, 0, 2, 3, 4)
    q_rows = ql.reshape(B, NA_HEADS, rows, nb, NA_QB, NA_DH).transpose(2, 0, 1, 3, 4, 5)
    n_win = key_idx.shape[-1]

    def row_attn(xs):
        qr, kidx, br = xs
        kg = kl[:, :, kidx]
        vg = vl[:, :, kidx]
        s_win = jnp.einsum('bhjqd,bhjkd->bhjqk', qr, kg).astype(jnp.float32) * scale + br[None]
        s_ctx = jnp.einsum('bhjqd,bhld->bhjql', qr, kc).astype(jnp.float32) * scale
        p = jax.nn.softmax(jnp.concatenate([s_win, s_ctx], -1), -1).astype(vl.dtype)
        return (jnp.einsum('bhjqk,bhjkd->bhjqd', p[..., :n_win], vg)
                + jnp.einsum('bhjql,bhld->bhjqd', p[..., n_win:], vc))

    o_rows = lax.map(row_attn, (q_rows, jnp.asarray(key_idx), bias))
    o_lat = o_rows.transpose(1, 0, 3, 4, 2, 5).reshape(B, N, D)
    return o_ctx @ wo, o_lat @ wo


def hgrn2_mixer(a_ctx, a_lat, wq, wi, wf, wg, norm_w, wo, lb):
    lb = lb.reshape(HG_HEADS, HG_DK)

    def proj(a):
        B, T, _ = a.shape
        q = jax.nn.silu(a @ wq).reshape(B, T, HG_HEADS, HG_DK) * HG_DK ** -0.5
        v = (a @ wi).reshape(B, T, HG_HEADS, HG_DV)
        dirs = []
        for d in range(2):
            z = (a @ wf[d]).reshape(B, T, HG_HEADS, HG_DK).astype(jnp.float32)
            f = lb + (1.0 - lb) * jax.nn.sigmoid(z)
            dirs.append((q, (1.0 - lb) * jax.nn.sigmoid(-z), v, jnp.log(f)))
        return dirs

    ctx_f, ctx_b = proj(a_ctx)
    lat_f, lat_b = proj(a_lat)
    s0 = jnp.zeros((a_lat.shape[0], HG_HEADS, HG_DK, HG_DV), jnp.float32)
    o_ctx, o_lat = bidirectional(chunk_gla, ctx_f, lat_f, ctx_b, lat_b, s0)

    def out(a, o):
        o = head_norm(o, norm_w)
        return (jax.nn.silu(a @ wg) * o).astype(a.dtype) @ wo

    return out(a_ctx, o_ctx), out(a_lat, o_lat)


def expert_choice_ffn(h, w_router, w1, w3, w2):
    B, n, D = h.shape
    cap = max(1, EC_CAPACITY * n // N_EXPERTS)
    aff = jax.nn.softmax(jnp.einsum('bnd,de->bne', h, w_router).astype(jnp.float32), axis=-1)
    gate, idx = lax.top_k(jnp.swapaxes(aff, 1, 2), cap)
    xg = jax.vmap(lambda hb, ib: hb[ib])(h, idx)
    u = jnp.einsum('becd,edf->becf', xg, w1)
    g = jnp.einsum('becd,edf->becf', xg, w3)
    y = jnp.einsum('becf,efd->becd', jax.nn.silu(u) * g, w2) * gate[..., None].astype(h.dtype)
    return jax.vmap(lambda yb, ib: jnp.zeros((n, D), yb.dtype).at[ib.reshape(-1)].add(yb.reshape(-1, D)))(y, idx)


def setup_inputs(seed: int = 0) -> dict:
    key = jax.random.key(seed)
    ks = iter(jax.random.split(key, 48))
    D = D_MODEL

    def nrm(shape, scale=1.0):
        return scale * jax.random.normal(next(ks), shape, jnp.float32)

    def dense(shape, fan_in, gain=1.0):
        return nrm(shape, gain * fan_in ** -0.5)

    gam = 1.0 - 2.0 ** (-5.0 - jnp.arange(RET_HEADS, dtype=jnp.float32))
    return {
        'x': nrm((BATCH, SEQ, D)),
        'c': nrm((BATCH, D)),
        'ctx': nrm((BATCH, CTX_LEN, D)),
        'c_ctx': nrm((D,)),
        'ada_w': dense((DEPTH, D, 6 * D), D, 0.5),
        'ada_b': nrm((DEPTH, 6 * D), 0.02),
        'norm_g': 1.0 + nrm((DEPTH, 2, D), 0.02),
        'final_g': 1.0 + nrm((D,), 0.02),
        'ret_wq': dense((D, RET_HEADS * RET_DK), D),
        'ret_wk': dense((D, RET_HEADS * RET_DK), D),
        'ret_wv': dense((D, RET_HEADS * RET_DV), D),
        'ret_wg': dense((D, RET_HEADS * RET_DV), D),
        'ret_wo': dense((RET_HEADS * RET_DV, D), RET_HEADS * RET_DV),
        'ret_decay': (jnp.log(gam) - jnp.log1p(-gam))[None, :] + nrm((2, RET_HEADS), 0.01),
        'ret_gn_w': 1.0 + nrm((D,), 0.02),
        'ret_gn_b': nrm((D,), 0.02),
        'ml_wq': dense((D, ML_HEADS * ML_DK), D),
        'ml_wk': dense((D, ML_HEADS * ML_DK), D),
        'ml_wv': dense((D, ML_HEADS * ML_DV), D),
        'ml_wog': dense((D, ML_HEADS * ML_DV), D),
        'ml_wgate': dense((2, D, 2 * ML_HEADS), D, 0.5),
        'ml_bgate': jnp.concatenate([nrm((2, ML_HEADS), 0.1),
                                     jnp.linspace(3.0, 6.0, ML_HEADS, dtype=jnp.float32)[None, :]
                                     + nrm((2, ML_HEADS), 0.01)], axis=-1),
        'ml_norm_w': 1.0 + nrm((D,), 0.02),
        'ml_wout': dense((ML_HEADS * ML_DV, D), ML_HEADS * ML_DV),
        'na_wqkv': dense((D, 3 * D), D),
        'na_rpb': nrm((NA_HEADS, 2 * NA_WIN_R - 1, 2 * NA_WIN_C - 1), 0.02),
        'na_wo': dense((D, D), D),
        'hg_wq': dense((D, HG_HEADS * HG_DK), D),
        'hg_wi': dense((D, HG_HEADS * HG_DV), D),
        'hg_wf': dense((2, D, HG_HEADS * HG_DK), D),
        'hg_wg': dense((D, HG_HEADS * HG_DV), D),
        'hg_norm_w': 1.0 + nrm((D,), 0.02),
        'hg_wo': dense((HG_HEADS * HG_DV, D), HG_HEADS * HG_DV),
        'hg_lb': nrm((DEPTH, HG_HEADS * HG_DK), 0.1),
        'moe_router': dense((DEPTH, D, N_EXPERTS), D),
        'moe_w1': dense((DEPTH, N_EXPERTS, D, EXPERT_FF), D),
        'moe_w3': dense((DEPTH, N_EXPERTS, D, EXPERT_FF), D),
        'moe_w2': dense((DEPTH, N_EXPERTS, EXPERT_FF, D), EXPERT_FF),
    }


def reference(x, c, ctx, c_ctx, ada_w, ada_b, norm_g, final_g,
              ret_wq, ret_wk, ret_wv, ret_wg, ret_wo, ret_decay, ret_gn_w, ret_gn_b,
              ml_wq, ml_wk, ml_wv, ml_wog, ml_wgate, ml_bgate, ml_norm_w, ml_wout,
              na_wqkv, na_rpb, na_wo,
              hg_wq, hg_wi, hg_wf, hg_wg, hg_norm_w, hg_wo, hg_lb,
              moe_router, moe_w1, moe_w3, moe_w2):
    h_lat, h_ctx = x, ctx
    s_lat = jax.nn.silu(c)
    s_ctx = jax.nn.silu(c_ctx)
    lb_all = jnp.cumsum(jax.nn.softmax(hg_lb.astype(jnp.float32), axis=0), axis=0)
    for i in range(DEPTH):
        last = i == DEPTH - 1
        sh1, sc1, g1, sh2, sc2, g2 = jnp.split((s_lat @ ada_w[i] + ada_b[i])[:, None, :], 6, axis=-1)
        csh1, csc1, cg1, csh2, csc2, cg2 = jnp.split(s_ctx @ ada_w[i] + ada_b[i], 6, axis=-1)
        a_lat = modulate(rmsnorm(h_lat, norm_g[i, 0]), sh1, sc1)
        a_ctx = modulate(rmsnorm(h_ctx, norm_g[i, 0]), csh1, csc1)
        kind = i % N_MIXERS
        if kind == 0:
            o_ctx, o_lat = retention_mixer(a_ctx, a_lat, ret_wq, ret_wk, ret_wv, ret_wg, ret_wo,
                                           ret_decay, ret_gn_w, ret_gn_b)
        elif kind == 1:
            o_ctx, o_lat = mlstm_mixer(a_ctx, a_lat, ml_wq, ml_wk, ml_wv, ml_wog, ml_wgate, ml_bgate,
                                       ml_norm_w, ml_wout)
        elif kind == 2:
            o_ctx, o_lat = na_mixer(a_ctx, a_lat, na_wqkv, na_rpb, na_wo)
        else:
            o_ctx, o_lat = hgrn2_mixer(a_ctx, a_lat, hg_wq, hg_wi, hg_wf, hg_wg, hg_norm_w, hg_wo,
                                       lb_all[i] - lb_all[0])
        h_lat = h_lat + g1 * o_lat.astype(h_lat.dtype)
        f_lat = modulate(rmsnorm(h_lat, norm_g[i, 1]), sh2, sc2)
        h_lat = h_lat + g2 * expert_choice_ffn(f_lat, moe_router[i], moe_w1[i], moe_w3[i], moe_w2[i])
        if not last:
            h_ctx = h_ctx + cg1 * o_ctx.astype(h_ctx.dtype)
            f_ctx = modulate(rmsnorm(h_ctx, norm_g[i, 1]), csh2, csc2)
            h_ctx = h_ctx + cg2 * expert_choice_ffn(f_ctx, moe_router[i], moe_w1[i], moe_w3[i], moe_w2[i])
    return rmsnorm(h_lat, final_g)
```

```python
import functools

import numpy as np
import jax
import jax.numpy as jnp
from jax import lax
from jax.experimental import pallas as pl
from jax.experimental.pallas import tpu as pltpu

GRID_W = 64
EPS = 1e-6
NEG = -1e30
CHUNK = 64
ROPE_THETA = 10000.0
RET_HEADS = 8
ML_HEADS = 8
NA_HEADS = 16
NA_WIN_R = 8
NA_WIN_C = 16
HG_HEADS = 16
N_EXPERTS = 16
EC_CAPACITY = 2
N_MIXERS = 4

VMEM_LIMIT_BYTES_V7X = 58 * 1024 * 1024
LANES_V7X = 128

F32 = jnp.float32
BF16 = jnp.bfloat16


def _cparams(sem):
    return pltpu.CompilerParams(dimension_semantics=sem, vmem_limit_bytes=VMEM_LIMIT_BYTES_V7X)


def _pick(n, prefs):
    for p in prefs:
        if n % p == 0:
            return p
    return n


def _dot(a, b):
    return jnp.dot(a, b, preferred_element_type=F32)


def _dot_nt(a, b):
    return lax.dot_general(a, b, (((1,), (1,)), ((), ())), preferred_element_type=F32)


def _dot_tn(a, b):
    return lax.dot_general(a, b, (((0,), (0,)), ((), ())), preferred_element_type=F32)


def _split3(x):
    x1 = x.astype(BF16)
    r1 = x - x1.astype(F32)
    x2 = r1.astype(BF16)
    r2 = r1 - x2.astype(F32)
    return x1, x2, r2.astype(BF16)


def _dot_exact01(a01, x):
    x1, x2, x3 = _split3(x)
    return _dot(a01, x1) + _dot(a01, x2) + _dot(a01, x3)


def _dot_nt_exact01(a01, x):
    x1, x2, x3 = _split3(x)
    return _dot_nt(a01, x1) + _dot_nt(a01, x2) + _dot_nt(a01, x3)


def _norm_mod(h, g, sh, sc):
    ms = jnp.mean(h * h, axis=-1, keepdims=True)
    y = h * lax.rsqrt(ms + EPS) * g
    return y * (1.0 + sc) + sh


def _mod_rows(mc_ref, ml_ref, row0, tm, L, k):
    row = row0 + lax.broadcasted_iota(jnp.int32, (tm, 1), 0)
    return jnp.where(row < L, mc_ref[k:k + 1, :], ml_ref[0, k:k + 1, :])


def _mod_kernel(c_ref, w_ref, b_ref, o_ref):
    c = c_ref[...]
    s = (c * jax.nn.sigmoid(c)).astype(BF16)
    o_ref[0] = _dot(s, w_ref[0].astype(BF16)) + b_ref[0]


def _mod_vectors(c, c_ctx, ada_w, ada_b):
    B, D = c.shape
    depth, _, n6 = ada_w.shape
    mp = -(-(B + 1) // 8) * 8
    c_all = jnp.concatenate([c, c_ctx[None], jnp.zeros((mp - B - 1, D), F32)], 0)
    tn = _pick(n6, (1024, 512, 256, 128))
    out = pl.pallas_call(
        _mod_kernel,
        out_shape=jax.ShapeDtypeStruct((depth, mp, n6), F32),
        grid=(depth, n6 // tn),
        in_specs=[pl.BlockSpec((mp, D), lambda i, j: (0, 0)),
                  pl.BlockSpec((1, D, tn), lambda i, j: (i, 0, j)),
                  pl.BlockSpec((1, 1, tn), lambda i, j: (i, 0, j))],
        out_specs=pl.BlockSpec((1, mp, tn), lambda i, j: (i, 0, j)),
        compiler_params=_cparams(("parallel", "parallel")),
        name="mod_vectors",
    )(c_all, ada_w, ada_b.reshape(depth, 1, n6))
    mod_lat = out[:, :B].reshape(depth, B, 6, D)
    mod_ctx = out[:, B].reshape(depth, 6, D)
    return mod_lat, mod_ctx


def _proj_kernel(h_ref, g_ref, mc_ref, ml_ref, w_ref, b_ref, o_ref, a_scr, *, L, tm):
    i = pl.program_id(1)

    @pl.when(pl.program_id(2) == 0)
    def _():
        sh = _mod_rows(mc_ref, ml_ref, i * tm, tm, L, 0)
        sc = _mod_rows(mc_ref, ml_ref, i * tm, tm, L, 1)
        a_scr[...] = _norm_mod(h_ref[0], g_ref[...], sh, sc).astype(BF16)

    o_ref[0] = _dot(a_scr[...], w_ref[...]) + b_ref[...]


def _proj(h, g, mc, ml, w, bias, L):
    B, T, D = h.shape
    n = w.shape[1]
    tm = _pick(T, (768, 576, 384, 256, 128, 64))
    tn = _pick(n, (1024, 896, 512, 256, 128))
    return pl.pallas_call(
        functools.partial(_proj_kernel, L=L, tm=tm),
        out_shape=jax.ShapeDtypeStruct((B, T, n), F32),
        grid=(B, T // tm, n // tn),
        in_specs=[pl.BlockSpec((1, tm, D), lambda b, i, j: (b, i, 0)),
                  pl.BlockSpec((1, D), lambda b, i, j: (0, 0)),
                  pl.BlockSpec((6, D), lambda b, i, j: (0, 0)),
                  pl.BlockSpec((1, 6, D), lambda b, i, j: (b, 0, 0)),
                  pl.BlockSpec((D, tn), lambda b, i, j: (0, j)),
                  pl.BlockSpec((1, tn), lambda b, i, j: (0, j))],
        out_specs=pl.BlockSpec((1, tm, tn), lambda b, i, j: (b, i, j)),
        scratch_shapes=[pltpu.VMEM((tm, D), BF16)],
        compiler_params=_cparams(("parallel", "parallel", "arbitrary")),
        name="proj",
    )(h, g.reshape(1, D), mc, ml, w, bias)


def _bwd_chunk(s, nctx, ntot):
    return jnp.where(s < nctx, nctx - 1 - s, ntot - 1 - (s - nctx))


def _tri(C):
    r = lax.broadcasted_iota(jnp.int32, (C, C), 0)
    c = lax.broadcasted_iota(jnp.int32, (C, C), 1)
    return r >= c, r <= c


def _gla_step(q, k, v, b, b_end, mask, s_ref, idx):
    q_in = (q * jnp.exp(b)).astype(BF16)
    k_in = (k * jnp.exp(-b)).astype(BF16)
    k_end = (k * jnp.exp(b_end - b)).astype(BF16)
    vb = v.astype(BF16)
    a = jnp.where(mask, _dot_nt(q_in, k_in), 0.0)
    st = s_ref[idx]
    o = _dot(a.astype(BF16), vb) + _dot_nt(q_in, st.astype(BF16))
    s_ref[idx] = jnp.exp(b_end) * st + _dot_tn(vb, k_end)
    return o


def _head_norm(o, w, b, center):
    if center:
        o = o - jnp.mean(o, axis=-1, keepdims=True)
    o = o * lax.rsqrt(jnp.mean(o * o, axis=-1, keepdims=True) + EPS)
    o = o * w
    if b is not None:
        o = o + b
    return o


def _ret_kernel(dec_ref, q_ref, k_ref, v_ref, g_ref, cos_ref, sin_ref, gw_ref, gb_ref, u_ref,
                s_scr, of_scr, ob_scr, *, dk, dv, C, nctx, ntot, scale):
    h = pl.program_id(1)
    half = dk // 2
    lower, upper = _tri(C)
    s_scr[...] = jnp.zeros(s_scr.shape, F32)
    cnt_f = (lax.broadcasted_iota(jnp.int32, (C, 1), 0) + 1).astype(F32)
    cnt_b = (C - lax.broadcasted_iota(jnp.int32, (C, 1), 0)).astype(F32)

    def load(rows):
        cos = cos_ref[rows, :]
        sin = sin_ref[rows, :]
        q = q_ref[0, rows, :]
        k = k_ref[0, rows, :]
        q1, q2 = q[:, :half], q[:, half:]
        k1, k2 = k[:, :half], k[:, half:]
        qr = jnp.concatenate([q1 * cos - q2 * sin, q1 * sin + q2 * cos], axis=1)
        kr = jnp.concatenate([k1 * cos - k2 * sin, k1 * sin + k2 * cos], axis=1) * scale
        return qr, kr, v_ref[0, rows, :]

    lg_f = jax.nn.log_sigmoid(jnp.full((1, 1), dec_ref[0, h], F32))
    lg_b = jax.nn.log_sigmoid(jnp.full((1, 1), dec_ref[1, h], F32))

    def body(s, carry):
        rf = pl.ds(pl.multiple_of(s * C, C), C)
        q, k, v = load(rf)
        of_scr[rf, :] = _gla_step(q, k, v, lg_f * cnt_f, lg_f * C, lower, s_scr, 0)
        rb = pl.ds(pl.multiple_of(_bwd_chunk(s, nctx, ntot) * C, C), C)
        q, k, v = load(rb)
        ob_scr[rb, :] = _gla_step(q, k, v, lg_b * cnt_b, lg_b * C, upper, s_scr, 1)
        return carry

    lax.fori_loop(0, ntot, body, 0)

    rb_sz = _pick(ntot * C, (256, 128, 64))

    def fin(i, carry):
        rows = pl.ds(pl.multiple_of(i * rb_sz, rb_sz), rb_sz)
        o = _head_norm(of_scr[rows, :] + ob_scr[rows, :], gw_ref[...], gb_ref[...], True)
        g = g_ref[0, rows, :]
        u_ref[0, rows, :] = (g * jax.nn.sigmoid(g) * o).astype(BF16)
        return carry

    lax.fori_loop(0, ntot * C // rb_sz, fin, 0)


def _rope_tables(N, L, dk):
    t = np.arange(N)
    row = (t // GRID_W).astype(np.float32)
    col = (t % GRID_W).astype(np.float32)
    quarter = dk // 4
    inv = (ROPE_THETA ** (-np.arange(quarter, dtype=np.float32) / quarter)).astype(np.float32)
    ang = np.concatenate([row[:, None] * inv, col[:, None] * inv], -1).astype(np.float32)
    cos = np.concatenate([np.ones((L, dk // 2), np.float32), np.cos(ang)], 0)
    sin = np.concatenate([np.zeros((L, dk // 2), np.float32), np.sin(ang)], 0)
    return jnp.asarray(cos, F32), jnp.asarray(sin, F32)


def _retention_core(y, decay, gn_w, gn_b, L):
    B, T, n4 = y.shape
    D = n4 // 4
    H = RET_HEADS
    dk = D // H
    dv = dk
    C = min(CHUNK, L)
    cos, sin = _rope_tables(T - L, L, dk)
    blk = lambda seg: pl.BlockSpec((1, T, dk), lambda b, h, seg=seg: (b, 0, seg * H + h))
    return pl.pallas_call(
        functools.partial(_ret_kernel, dk=dk, dv=dv, C=C, nctx=L // C, ntot=T // C, scale=dk ** -0.5),
        out_shape=jax.ShapeDtypeStruct((B, T, D), BF16),
        grid=(B, H),
        in_specs=[pl.BlockSpec(memory_space=pltpu.SMEM),
                  blk(0), blk(1), blk(2), blk(3),
                  pl.BlockSpec((T, dk // 2), lambda b, h: (0, 0)),
                  pl.BlockSpec((T, dk // 2), lambda b, h: (0, 0)),
                  pl.BlockSpec((1, dv), lambda b, h: (0, h)),
                  pl.BlockSpec((1, dv), lambda b, h: (0, h))],
        out_specs=pl.BlockSpec((1, T, dv), lambda b, h: (b, 0, h)),
        scratch_shapes=[pltpu.VMEM((2, dv, dk), F32), pltpu.VMEM((T, dv), F32), pltpu.VMEM((T, dv), F32)],
        compiler_params=_cparams(("parallel", "parallel")),
        name="retention_core",
    )(decay.astype(F32), y, y, y, y, cos, sin, gn_w.reshape(1, D), gn_b.reshape(1, D))


def _hg_kernel(q_ref, i_ref, zf_ref, zb_ref, g_ref, lb_ref, nw_ref, u_ref,
               s_scr, of_scr, ob_scr, *, hp, dk, dv, C, nctx, ntot, scale):
    lower, upper = _tri(C)
    lo_b, up_b = lower.astype(BF16), upper.astype(BF16)
    s_scr[...] = jnp.zeros(s_scr.shape, F32)

    def chunk(rows, hh, z_ref, tri_b, mask, end_row, o_scr, idx):
        cs = slice(hh * dk, (hh + 1) * dk)
        lb = lb_ref[:, cs]
        qr = q_ref[0, rows, cs]
        q = qr * jax.nn.sigmoid(qr) * scale
        z = z_ref[0, rows, cs]
        f = lb + (1.0 - lb) * jax.nn.sigmoid(z)
        k = (1.0 - lb) * jax.nn.sigmoid(-z)
        b = _dot_exact01(tri_b, jnp.log(f))
        b_end = b[end_row:end_row + 1, :]
        o_scr[rows, hh * dv:(hh + 1) * dv] = _gla_step(q, k, i_ref[0, rows, hh * dv:(hh + 1) * dv], b, b_end, mask,
                                                       s_scr, idx)

    def body(s, carry):
        rf = pl.ds(pl.multiple_of(s * C, C), C)
        rb = pl.ds(pl.multiple_of(_bwd_chunk(s, nctx, ntot) * C, C), C)
        for hh in range(hp):
            chunk(rf, hh, zf_ref, lo_b, lower, C - 1, of_scr, 2 * hh)
            chunk(rb, hh, zb_ref, up_b, upper, 0, ob_scr, 2 * hh + 1)
        return carry

    lax.fori_loop(0, ntot, body, 0)

    rb_sz = _pick(ntot * C, (256, 128, 64))

    def fin(i, carry):
        rows = pl.ds(pl.multiple_of(i * rb_sz, rb_sz), rb_sz)
        for hh in range(hp):
            cs = slice(hh * dv, (hh + 1) * dv)
            o = _head_norm(of_scr[rows, cs] + ob_scr[rows, cs], nw_ref[:, cs], None, False)
            g = g_ref[0, rows, cs]
            u_ref[0, rows, cs] = (g * jax.nn.sigmoid(g) * o).astype(BF16)
        return carry

    lax.fori_loop(0, ntot * C // rb_sz, fin, 0)


def _hgrn2_core(y, lb, norm_w, L):
    B, T, n5 = y.shape
    D = n5 // 5
    H = HG_HEADS
    dk = D // H
    dv = dk
    hp = 2 if H % 2 == 0 else 1
    G = H // hp
    C = min(CHUNK, L)
    blk = lambda seg: pl.BlockSpec((1, T, hp * dk), lambda b, h, seg=seg: (b, 0, seg * G + h))
    return pl.pallas_call(
        functools.partial(_hg_kernel, hp=hp, dk=dk, dv=dv, C=C, nctx=L // C, ntot=T // C, scale=dk ** -0.5),
        out_shape=jax.ShapeDtypeStruct((B, T, D), BF16),
        grid=(B, G),
        in_specs=[blk(0), blk(1), blk(2), blk(3), blk(4),
                  pl.BlockSpec((1, hp * dk), lambda b, h: (0, h)),
                  pl.BlockSpec((1, hp * dv), lambda b, h: (0, h))],
        out_specs=pl.BlockSpec((1, T, hp * dv), lambda b, h: (b, 0, h)),
        scratch_shapes=[pltpu.VMEM((2 * hp, dv, dk), F32), pltpu.VMEM((T, hp * dv), F32),
                        pltpu.VMEM((T, hp * dv), F32)],
        compiler_params=_cparams(("parallel", "parallel")),
        name="hgrn2_core",
    )(y, y, y, y, y, lb.reshape(1, D).astype(F32), norm_w.reshape(1, D))


def _ml_kernel(q_ref, k_ref, v_ref, og_ref, gt_ref, nw_ref, u_ref, c_scr, n_scr, m_scr, of_scr, ob_scr,
               *, H, dk, dv, C, nctx, ntot, scale):
    h = pl.program_id(1)
    lower, upper = _tri(C)
    lo_b, up_b = lower.astype(BF16), upper.astype(BF16)
    c_scr[...] = jnp.zeros(c_scr.shape, F32)
    n_scr[...] = jnp.zeros(n_scr.shape, F32)
    m_scr[...] = jnp.zeros(m_scr.shape, F32)
    gw = gt_ref.shape[2]
    lane = lax.broadcasted_iota(jnp.int32, (C, gw), 1)
    eye8 = (lax.broadcasted_iota(jnp.int32, (8, gw), 0) == lax.broadcasted_iota(jnp.int32, (8, gw), 1)).astype(BF16)

    def chunk(rows, d, tri_b, mask, end_row, o_scr):
        gt = gt_ref[0, rows, :]
        cum = _dot_exact01(tri_b, jax.nn.log_sigmoid(gt))
        b_col = jnp.sum(jnp.where(lane == h + H + 2 * H * d, cum, 0.0), axis=1, keepdims=True)
        i_col = jnp.sum(jnp.where(lane == h + 2 * H * d, gt, 0.0), axis=1, keepdims=True)
        z = jnp.where(lane == 0, b_col, jnp.where(lane == 1, i_col, 0.0))
        r = _dot_nt_exact01(eye8, z)
        b_row, i_row = r[0:1, :], r[1:2, :]
        q = q_ref[0, rows, :]
        k = k_ref[0, rows, :] * scale
        v = v_ref[0, rows, :]
        m_prev = m_scr[d, 0:1, 0:1]
        logw = jnp.where(mask, b_col - b_row + i_row, -jnp.inf)
        log_carry = b_col + m_prev
        m_i = jnp.maximum(log_carry, jnp.max(logw, axis=1, keepdims=True))
        qb, vb = q.astype(BF16), v.astype(BF16)
        s = _dot_nt(qb, k.astype(BF16)) * jnp.exp(logw - m_i)
        a = jnp.exp(log_carry - m_i)
        cst = c_scr[d]
        nst = n_scr[d]
        num = _dot(s.astype(BF16), vb) + a * _dot(qb, cst.astype(BF16))
        den = jnp.sum(s, axis=1, keepdims=True) + a * jnp.sum(q * nst, axis=1, keepdims=True)
        o_scr[rows, :] = num / jnp.maximum(jnp.abs(den), jnp.exp(-m_i))
        b_end = b_col[end_row:end_row + 1, :]
        logw_end = b_end - b_col + i_col
        m_new = jnp.maximum(b_end + m_prev, jnp.max(logw_end, axis=0, keepdims=True))
        decay = jnp.exp(b_end + m_prev - m_new)
        kw = k * jnp.exp(logw_end - m_new)
        c_scr[d] = decay * cst + _dot_tn(kw.astype(BF16), vb)
        n_scr[d] = decay * nst + jnp.sum(kw, axis=0, keepdims=True)
        m_scr[d] = jnp.broadcast_to(m_new, m_scr.shape[1:])

    def body(s, carry):
        rf = pl.ds(pl.multiple_of(s * C, C), C)
        rb = pl.ds(pl.multiple_of(_bwd_chunk(s, nctx, ntot) * C, C), C)
        chunk(rf, 0, lo_b, lower, C - 1, of_scr)
        chunk(rb, 1, up_b, upper, 0, ob_scr)
        return carry

    lax.fori_loop(0, ntot, body, 0)

    rb_sz = _pick(ntot * C, (256, 128, 64))

    def fin(i, carry):
        rows = pl.ds(pl.multiple_of(i * rb_sz, rb_sz), rb_sz)
        o = _head_norm(of_scr[rows, :] + ob_scr[rows, :], nw_ref[...], None, True)
        u_ref[0, rows, :] = (jax.nn.sigmoid(og_ref[0, rows, :]) * o).astype(BF16)
        return carry

    lax.fori_loop(0, ntot * C // rb_sz, fin, 0)


def _mlstm_core(y, norm_w, D, L):
    B, T, n = y.shape
    H = ML_HEADS
    dv = D // H
    dk = dv // 2
    gw = n - 3 * D
    C = min(CHUNK, L)
    return pl.pallas_call(
        functools.partial(_ml_kernel, H=H, dk=dk, dv=dv, C=C, nctx=L // C, ntot=T // C, scale=dk ** -0.5),
        out_shape=jax.ShapeDtypeStruct((B, T, D), BF16),
        grid=(B, H),
        in_specs=[pl.BlockSpec((1, T, dk), lambda b, h: (b, 0, h)),
                  pl.BlockSpec((1, T, dk), lambda b, h: (b, 0, H + h)),
                  pl.BlockSpec((1, T, dv), lambda b, h: (b, 0, H + h)),
                  pl.BlockSpec((1, T, dv), lambda b, h: (b, 0, 2 * H + h)),
                  pl.BlockSpec((1, T, gw), lambda b, h: (b, 0, 3 * D // gw)),
                  pl.BlockSpec((1, dv), lambda b, h: (0, h))],
        out_specs=pl.BlockSpec((1, T, dv), lambda b, h: (b, 0, h)),
        scratch_shapes=[pltpu.VMEM((2, dk, dv), F32), pltpu.VMEM((2, 1, dk), F32), pltpu.VMEM((2, 8, LANES_V7X), F32),
                        pltpu.VMEM((T, dv), F32), pltpu.VMEM((T, dv), F32)],
        compiler_params=_cparams(("parallel", "parallel")),
        name="mlstm_core",
    )(y, y, y, y, y, norm_w.reshape(1, D))


def _na_kernel(q_ref, k_ref, v_ref, bias_ref, o_ref, *, L, rows, wr, scale):
    W = GRID_W
    kc = k_ref[0, 0:L, :].astype(BF16)
    vc = v_ref[0, 0:L, :].astype(BF16)
    qc = q_ref[0, 0:L, :].astype(BF16)
    s = _dot_nt(qc, kc) * scale
    p = jnp.exp(s - jnp.max(s, axis=1, keepdims=True))
    o_ref[0, 0:L, :] = (_dot(p.astype(BF16), vc) / jnp.sum(p, axis=1, keepdims=True)).astype(BF16)

    def body(r, carry):
        r0 = jnp.clip(r - wr // 2, 0, rows - wr)
        qrows = pl.ds(pl.multiple_of(L + r * W, W), W)
        krows = pl.ds(pl.multiple_of(L + r0 * W, W), wr * W)
        q = q_ref[0, qrows, :].astype(BF16)
        kw = k_ref[0, krows, :].astype(BF16)
        vw = v_ref[0, krows, :].astype(BF16)
        s_win = _dot_nt(q, kw) * scale + bias_ref[0, r - r0]
        s_ctx = _dot_nt(q, kc) * scale
        m = jnp.maximum(jnp.max(s_win, axis=1, keepdims=True), jnp.max(s_ctx, axis=1, keepdims=True))
        p_win = jnp.exp(s_win - m)
        p_ctx = jnp.exp(s_ctx - m)
        den = jnp.sum(p_win, axis=1, keepdims=True) + jnp.sum(p_ctx, axis=1, keepdims=True)
        o = _dot(p_win.astype(BF16), vw) + _dot(p_ctx.astype(BF16), vc)
        o_ref[0, qrows, :] = (o / den).astype(BF16)
        return carry

    lax.fori_loop(0, rows, body, 0, unroll=2)


def _na_bias_index(rows):
    wr = min(NA_WIN_R, rows)
    W = GRID_W
    q = np.arange(W)
    c0 = np.clip(q - NA_WIN_C // 2, 0, W - NA_WIN_C)
    kr = np.arange(wr)
    kcol = np.arange(W)
    dc = np.clip(kcol[None, :] - q[:, None], -(NA_WIN_C - 1), NA_WIN_C - 1)
    ok = (kcol[None, :] >= c0[:, None]) & (kcol[None, :] < c0[:, None] + NA_WIN_C)
    s = np.arange(wr)
    dr = kr[None, :] - s[:, None]
    idx = (dr[:, None, :, None] + NA_WIN_R - 1) * (2 * NA_WIN_C - 1) + (dc[None, :, None, :] + NA_WIN_C - 1)
    okf = np.broadcast_to(ok[None, :, None, :], idx.shape)
    return idx.reshape(wr, W, wr * W).astype(np.int32), okf.reshape(wr, W, wr * W), wr


def _na_core(y, rpb, L):
    B, T, n3 = y.shape
    D = n3 // 3
    H = NA_HEADS
    dh = D // H
    rows = (T - L) // GRID_W
    idx, ok, wr = _na_bias_index(rows)
    bias = jnp.where(jnp.asarray(ok)[None], rpb.reshape(H, -1).astype(F32)[:, idx], NEG)
    blk = lambda seg: pl.BlockSpec((1, T, dh), lambda b, h, seg=seg: (b, 0, seg * H + h))
    return pl.pallas_call(
        functools.partial(_na_kernel, L=L, rows=rows, wr=wr, scale=dh ** -0.5),
        out_shape=jax.ShapeDtypeStruct((B, T, D), BF16),
        grid=(B, H),
        in_specs=[blk(0), blk(1), blk(2),
                  pl.BlockSpec((1, wr, GRID_W, wr * GRID_W), lambda b, h: (h, 0, 0, 0))],
        out_specs=pl.BlockSpec((1, T, dh), lambda b, h: (b, 0, h)),
        compiler_params=_cparams(("parallel", "parallel")),
        name="na_core",
    )(y, y, y, bias)


def _outproj_kernel(u_ref, wo_ref, h_ref, g_ref, mc_ref, ml_ref, wr_ref, h1_ref, f_ref, lg_ref, *, L, tm):
    row0 = pl.program_id(1) * tm
    g1 = _mod_rows(mc_ref, ml_ref, row0, tm, L, 2)
    sh2 = _mod_rows(mc_ref, ml_ref, row0, tm, L, 3)
    sc2 = _mod_rows(mc_ref, ml_ref, row0, tm, L, 4)
    h1 = h_ref[0] + g1 * _dot(u_ref[0], wo_ref[...])
    h1_ref[0] = h1
    f = _norm_mod(h1, g_ref[...], sh2, sc2)
    f_ref[0] = f.astype(BF16)
    lg_ref[0] = jnp.dot(f, wr_ref[...], preferred_element_type=F32, precision=lax.Precision.HIGHEST)


def _outproj(u, wo, h, g, mc, ml, w_router, L):
    B, T, D = h.shape
    E = w_router.shape[1]
    tm = _pick(T, (384, 256, 128, 64))
    return pl.pallas_call(
        functools.partial(_outproj_kernel, L=L, tm=tm),
        out_shape=(jax.ShapeDtypeStruct((B, T, D), F32), jax.ShapeDtypeStruct((B, T, D), BF16),
                   jax.ShapeDtypeStruct((B, T, E), F32)),
        grid=(B, T // tm),
        in_specs=[pl.BlockSpec((1, tm, D), lambda b, i: (b, i, 0)),
                  pl.BlockSpec((D, D), lambda b, i: (0, 0)),
                  pl.BlockSpec((1, tm, D), lambda b, i: (b, i, 0)),
                  pl.BlockSpec((1, D), lambda b, i: (0, 0)),
                  pl.BlockSpec((6, D), lambda b, i: (0, 0)),
                  pl.BlockSpec((1, 6, D), lambda b, i: (b, 0, 0)),
                  pl.BlockSpec((D, E), lambda b, i: (0, 0))],
        out_specs=(pl.BlockSpec((1, tm, D), lambda b, i: (b, i, 0)),
                   pl.BlockSpec((1, tm, D), lambda b, i: (b, i, 0)),
                   pl.BlockSpec((1, tm, E), lambda b, i: (b, i, 0))),
        compiler_params=_cparams(("parallel", "parallel")),
        name="outproj",
    )(u, wo, h, g.reshape(1, D), mc, ml, w_router)


def _route_kernel(lg_ref, slot_ref, aff_ref, before_scr, *, L, caps):
    x = lg_ref[...]
    B, E, T = x.shape
    e = jnp.exp(x - jnp.max(x, axis=1, keepdims=True))
    aff = e / jnp.sum(e, axis=1, keepdims=True)
    aff_ref[...] = aff
    nmax = before_scr.shape[0]
    rb = _pick(nmax, (256, 128, 64))

    def fill(i, carry):
        r = i * rb + lax.broadcasted_iota(jnp.int32, (rb, nmax), 0)
        before_scr[pl.ds(pl.multiple_of(i * rb, rb), rb), :] = jnp.where(
            r < lax.broadcasted_iota(jnp.int32, (rb, nmax), 1), 1.0, 0.0).astype(BF16)
        return carry

    lax.fori_loop(0, nmax // rb, fill, 0)
    base = 0
    for lo, hi, cap in ((L, T, caps[1]), (0, L, caps[0])):
        n = hi - lo
        before = before_scr[0:n, 0:n]
        a = aff[:, :, lo:hi].reshape(B * E, n)
        bits = lax.bitcast_convert_type(a, jnp.int32)

        def search(it, thr, bits=bits, cap=cap):
            cand = thr | lax.shift_left(jnp.int32(1), 30 - it)
            cnt = jnp.sum(jnp.where(bits >= cand, 1.0, 0.0), axis=1, keepdims=True)
            return jnp.where(cnt >= cap, cand, thr)

        thr = lax.fori_loop(0, 31, search, jnp.zeros((B * E, 1), jnp.int32))
        gt = bits > thr
        eq = bits == thr
        need = cap - jnp.sum(jnp.where(gt, 1.0, 0.0), axis=1, keepdims=True)
        rank_eq = _dot(jnp.where(eq, 1.0, 0.0).astype(BF16), before)
        sel = gt | (eq & (rank_eq < need))
        pos = _dot(jnp.where(sel, 1.0, 0.0).astype(BF16), before)
        slot = jnp.where(sel, pos.astype(jnp.int32) + base, -1)
        slot_ref[:, :, lo:hi] = slot.reshape(B, E, n)
        base += cap


def _route(logits_t, L, caps):
    B, E, T = logits_t.shape
    return pl.pallas_call(
        functools.partial(_route_kernel, L=L, caps=caps),
        out_shape=(jax.ShapeDtypeStruct((B, E, T), jnp.int32), jax.ShapeDtypeStruct((B, E, T), F32)),
        scratch_shapes=[pltpu.VMEM((max(L, T - L), max(L, T - L)), BF16)],
        compiler_params=pltpu.CompilerParams(vmem_limit_bytes=VMEM_LIMIT_BYTES_V7X),
        name="route",
    )(logits_t)


def _gather_kernel(slot_ref, aff_ref, f_ref, x_ref, gate_ref, *, R):
    T = f_ref.shape[1]
    hit = lax.broadcasted_iota(jnp.int32, (R, T), 0) == slot_ref[0, 0]
    x_ref[0, 0] = _dot(jnp.where(hit, 1.0, 0.0).astype(BF16), f_ref[0]).astype(BF16)
    gate = jnp.sum(jnp.where(hit, aff_ref[0, 0], 0.0), axis=1, keepdims=True)
    gate_ref[0, 0] = jnp.broadcast_to(gate, gate_ref.shape[2:])


def _gather(slot, aff, f, R):
    B, E, T = slot.shape
    D = f.shape[2]
    return pl.pallas_call(
        functools.partial(_gather_kernel, R=R),
        out_shape=(jax.ShapeDtypeStruct((B, E, R, D), BF16), jax.ShapeDtypeStruct((B, E, R, LANES_V7X), F32)),
        grid=(B, E),
        in_specs=[pl.BlockSpec((1, 1, 1, T), lambda b, e: (b, e, 0, 0)),
                  pl.BlockSpec((1, 1, 1, T), lambda b, e: (b, e, 0, 0)),
                  pl.BlockSpec((1, T, D), lambda b, e: (b, 0, 0))],
        out_specs=(pl.BlockSpec((1, 1, R, D), lambda b, e: (b, e, 0, 0)),
                   pl.BlockSpec((1, 1, R, LANES_V7X), lambda b, e: (b, e, 0, 0))),
        compiler_params=_cparams(("parallel", "arbitrary")),
        name="moe_gather",
    )(slot.reshape(B, E, 1, T), aff.reshape(B, E, 1, T), f)


def _ffn_kernel(x_ref, gate_ref, w1_ref, w3_ref, w2_ref, y_ref):
    bs, _, R, D = x_ref.shape
    x = x_ref[...].reshape(bs * R, D)
    u = _dot(x, w1_ref[0])
    g = _dot(x, w3_ref[0])
    hm = (u * jax.nn.sigmoid(u) * g).astype(BF16)
    gate = gate_ref[...].reshape(bs * R, gate_ref.shape[3])[:, 0:1]
    y_ref[...] = (_dot(hm, w2_ref[0]) * gate).astype(BF16).reshape(bs, 1, R, D)


def _ffn(xg, gate, w1, w3, w2):
    B, E, R, D = xg.shape
    FF = w1.shape[2]
    bs = _pick(B, (2, 1))
    return pl.pallas_call(
        _ffn_kernel,
        out_shape=jax.ShapeDtypeStruct((B, E, R, D), BF16),
        grid=(E, B // bs),
        in_specs=[pl.BlockSpec((bs, 1, R, D), lambda e, b: (b, e, 0, 0)),
                  pl.BlockSpec((bs, 1, R, LANES_V7X), lambda e, b: (b, e, 0, 0)),
                  pl.BlockSpec((1, D, FF), lambda e, b: (e, 0, 0)),
                  pl.BlockSpec((1, D, FF), lambda e, b: (e, 0, 0)),
                  pl.BlockSpec((1, FF, D), lambda e, b: (e, 0, 0))],
        out_specs=pl.BlockSpec((bs, 1, R, D), lambda e, b: (b, e, 0, 0)),
        compiler_params=_cparams(("parallel", "arbitrary")),
        name="moe_ffn",
    )(xg, gate, w1, w3, w2)


def _combine_kernel(slotc_ref, y_ref, h1_ref, mc_ref, ml_ref, fg_ref, o_ref, acc_scr, *, L, tt, R, final):
    e = pl.program_id(2)
    E = pl.num_programs(2)
    row0 = pl.program_id(1) * tt
    sc = slotc_ref[0]
    lane = lax.broadcasted_iota(jnp.int32, sc.shape, 1)
    col = jnp.sum(jnp.where(lane == e, sc, 0.0), axis=1, keepdims=True)
    hit = col == lax.broadcasted_iota(jnp.int32, (1, R), 1).astype(F32)
    contrib = _dot(jnp.where(hit, 1.0, 0.0).astype(BF16), y_ref[0, 0])

    @pl.when(e == 0)
    def _():
        acc_scr[...] = contrib

    @pl.when(e > 0)
    def _():
        acc_scr[...] += contrib

    @pl.when(e == E - 1)
    def _():
        g2 = _mod_rows(mc_ref, ml_ref, row0, tt, L, 5)
        out = h1_ref[0] + g2 * acc_scr[...]
        if final:
            out = out * lax.rsqrt(jnp.mean(out * out, axis=-1, keepdims=True) + EPS) * fg_ref[...]
        o_ref[0] = out


def _combine(slot_c, yff, h1, mc, ml, final_g, L, final):
    B, T, D = h1.shape
    E, R = yff.shape[1], yff.shape[2]
    tt = _pick(T, (768, 576, 384, 256, 128, 64))
    return pl.pallas_call(
        functools.partial(_combine_kernel, L=L, tt=tt, R=R, final=final),
        out_shape=jax.ShapeDtypeStruct((B, T, D), F32),
        grid=(B, T // tt, E),
        in_specs=[pl.BlockSpec((1, tt, E), lambda b, i, e: (b, i, 0)),
                  pl.BlockSpec((1, 1, R, D), lambda b, i, e: (b, e, 0, 0)),
                  pl.BlockSpec((1, tt, D), lambda b, i, e: (b, i, 0)),
                  pl.BlockSpec((6, D), lambda b, i, e: (0, 0)),
                  pl.BlockSpec((1, 6, D), lambda b, i, e: (b, 0, 0)),
                  pl.BlockSpec((1, D), lambda b, i, e: (0, 0))],
        out_specs=pl.BlockSpec((1, tt, D), lambda b, i, e: (b, i, 0)),
        scratch_shapes=[pltpu.VMEM((tt, D), F32)],
        compiler_params=_cparams(("parallel", "parallel", "arbitrary")),
        name="moe_combine",
    )(slot_c, yff, h1, mc, ml, final_g.reshape(1, D))


def _moe(f, logits, h1, mc, ml, w1, w3, w2, final_g, L, final):
    B, T, D = h1.shape
    caps = tuple(max(1, EC_CAPACITY * n // N_EXPERTS) for n in (L, T - L))
    slot, aff = _route(jnp.swapaxes(logits, 1, 2), L, caps)
    xg, gate = _gather(slot, aff, f, caps[0] + caps[1])
    yff = _ffn(xg, gate, w1, w3, w2)
    slot_c = jnp.swapaxes(slot, 1, 2).astype(F32)
    return _combine(slot_c, yff, h1, mc, ml, final_g, L, final)


def kernel(x, c, ctx, c_ctx, ada_w, ada_b, norm_g, final_g, ret_wq, ret_wk, ret_wv, ret_wg, ret_wo, ret_decay, ret_gn_w, ret_gn_b, ml_wq, ml_wk, ml_wv, ml_wog, ml_wgate, ml_bgate, ml_norm_w, ml_wout, na_wqkv, na_rpb, na_wo, hg_wq, hg_wi, hg_wf, hg_wg, hg_norm_w, hg_wo, hg_lb, moe_router, moe_w1, moe_w3, moe_w2):
    B, N, D = x.shape
    L = ctx.shape[1]
    depth = ada_w.shape[0]
    h = jnp.concatenate([ctx, x], axis=1)
    mod_lat, mod_ctx = _mod_vectors(c, c_ctx, ada_w, ada_b)
    lb_all = jnp.cumsum(jax.nn.softmax(hg_lb.astype(F32), axis=0), axis=0)
    zero_b = lambda n: jnp.zeros((1, n), F32)
    gpad = LANES_V7X - 4 * ML_HEADS
    for i in range(depth):
        mc, ml = mod_ctx[i], mod_lat[i]
        kind = i % N_MIXERS
        if kind == 0:
            w = jnp.concatenate([ret_wq, ret_wk, ret_wv, ret_wg], 1).astype(BF16)
            y = _proj(h, norm_g[i, 0], mc, ml, w, zero_b(w.shape[1]), L)
            u = _retention_core(y, ret_decay, ret_gn_w, ret_gn_b, L)
            wo = ret_wo
        elif kind == 1:
            w = jnp.concatenate([ml_wq, ml_wk, ml_wv, ml_wog, ml_wgate[0], ml_wgate[1], jnp.zeros((D, gpad), F32)],
                                1).astype(BF16)
            bias = jnp.concatenate([jnp.zeros((3 * D,), F32), ml_bgate[0], ml_bgate[1], jnp.zeros((gpad,), F32)])[None]
            y = _proj(h, norm_g[i, 0], mc, ml, w, bias, L)
            u = _mlstm_core(y, ml_norm_w, D, L)
            wo = ml_wout
        elif kind == 2:
            w = na_wqkv.astype(BF16)
            y = _proj(h, norm_g[i, 0], mc, ml, w, zero_b(w.shape[1]), L)
            u = _na_core(y, na_rpb, L)
            wo = na_wo
        else:
            w = jnp.concatenate([hg_wq, hg_wi, hg_wf[0], hg_wf[1], hg_wg], 1).astype(BF16)
            y = _proj(h, norm_g[i, 0], mc, ml, w, zero_b(w.shape[1]), L)
            u = _hgrn2_core(y, lb_all[i] - lb_all[0], hg_norm_w, L)
            wo = hg_wo
        h1, f, logits = _outproj(u, wo.astype(BF16), h, norm_g[i, 1], mc, ml, moe_router[i], L)
        h = _moe(f, logits, h1, mc, ml, moe_w1[i].astype(BF16), moe_w3[i].astype(BF16), moe_w2[i].astype(BF16),
                 final_g, L, i == depth - 1)
    return h[:, L:]
```

```python
import functools

import numpy as np
import jax
import jax.numpy as jnp
from jax import lax
from jax.experimental import pallas as pl
from jax.experimental.pallas import tpu as pltpu

GRID_W = 64
EPS = 1e-6
NEG = -1e30
CHUNK = 64
ROPE_THETA = 10000.0
RET_HEADS = 8
ML_HEADS = 8
NA_HEADS = 16
NA_WIN_R = 8
NA_WIN_C = 16
HG_HEADS = 16
N_EXPERTS = 16
EC_CAPACITY = 2
N_MIXERS = 4

VMEM_LIMIT_BYTES_V7X = 58 * 1024 * 1024
LANES_V7X = 128

F32 = jnp.float32
BF16 = jnp.bfloat16


def _cparams(sem):
    return pltpu.CompilerParams(dimension_semantics=sem, vmem_limit_bytes=VMEM_LIMIT_BYTES_V7X)


def _pick(n, prefs):
    for p in prefs:
        if n % p == 0:
            return p
    return n


def _dot(a, b):
    return jnp.dot(a, b, preferred_element_type=F32)


def _dot_nt(a, b):
    return lax.dot_general(a, b, (((1,), (1,)), ((), ())), preferred_element_type=F32)


def _dot_tn(a, b):
    return lax.dot_general(a, b, (((0,), (0,)), ((), ())), preferred_element_type=F32)


def _split3(x):
    x1 = x.astype(BF16)
    r1 = x - x1.astype(F32)
    x2 = r1.astype(BF16)
    r2 = r1 - x2.astype(F32)
    return x1, x2, r2.astype(BF16)


def _dot_exact01(a01, x):
    x1, x2, x3 = _split3(x)
    return _dot(a01, x1) + _dot(a01, x2) + _dot(a01, x3)


def _dot_nt_exact01(a01, x):
    x1, x2, x3 = _split3(x)
    return _dot_nt(a01, x1) + _dot_nt(a01, x2) + _dot_nt(a01, x3)


def _norm_mod(h, g, sh, sc):
    ms = jnp.mean(h * h, axis=-1, keepdims=True)
    y = h * lax.rsqrt(ms + EPS) * g
    return y * (1.0 + sc) + sh


def _mod_rows(mc_ref, ml_ref, row0, tm, L, k):
    row = row0 + lax.broadcasted_iota(jnp.int32, (tm, 1), 0)
    return jnp.where(row < L, mc_ref[k:k + 1, :], ml_ref[0, k:k + 1, :])


def _seq_block(T, L):
    cb = _pick(L, (256, 128, 64))
    assert L % cb == 0 and (T - L) % cb == 0
    return cb


def _bwd_block(s, nctx, ntot):
    return jnp.where(s < nctx, nctx - 1 - s, ntot - 1 - (s - nctx))


def _mod_kernel(c_ref, w_ref, b_ref, o_ref):
    c = c_ref[...]
    s = (c * jax.nn.sigmoid(c)).astype(BF16)
    o_ref[0] = _dot(s, w_ref[0].astype(BF16)) + b_ref[0]


def _mod_vectors(c, c_ctx, ada_w, ada_b):
    B, D = c.shape
    depth, _, n6 = ada_w.shape
    mp = -(-(B + 1) // 8) * 8
    c_all = jnp.concatenate([c, c_ctx[None], jnp.zeros((mp - B - 1, D), F32)], 0)
    tn = _pick(n6, (1024, 512, 256, 128))
    out = pl.pallas_call(
        _mod_kernel,
        out_shape=jax.ShapeDtypeStruct((depth, mp, n6), F32),
        grid=(depth, n6 // tn),
        in_specs=[pl.BlockSpec((mp, D), lambda i, j: (0, 0)),
                  pl.BlockSpec((1, D, tn), lambda i, j: (i, 0, j)),
                  pl.BlockSpec((1, 1, tn), lambda i, j: (i, 0, j))],
        out_specs=pl.BlockSpec((1, mp, tn), lambda i, j: (i, 0, j)),
        compiler_params=_cparams(("parallel", "parallel")),
        name="mod_vectors",
    )(c_all, ada_w, ada_b.reshape(depth, 1, n6))
    mod_lat = out[:, :B].reshape(depth, B, 6, D)
    mod_ctx = out[:, B].reshape(depth, 6, D)
    return mod_lat, mod_ctx


def _proj_kernel(h_ref, g_ref, mc_ref, ml_ref, w_ref, b_ref, o_ref, a_scr, *, L, tm):
    i = pl.program_id(1)

    @pl.when(pl.program_id(2) == 0)
    def _():
        sh = _mod_rows(mc_ref, ml_ref, i * tm, tm, L, 0)
        sc = _mod_rows(mc_ref, ml_ref, i * tm, tm, L, 1)
        a_scr[...] = _norm_mod(h_ref[0], g_ref[...], sh, sc).astype(BF16)

    o_ref[0] = _dot(a_scr[...], w_ref[...]) + b_ref[...]


def _proj(h, g, mc, ml, w, bias, L):
    B, T, D = h.shape
    n = w.shape[1]
    tm = _pick(T, (768, 576, 384, 256, 128, 64))
    tn = _pick(n, (1024, 896, 512, 256, 128))
    return pl.pallas_call(
        functools.partial(_proj_kernel, L=L, tm=tm),
        out_shape=jax.ShapeDtypeStruct((B, T, n), F32),
        grid=(B, T // tm, n // tn),
        in_specs=[pl.BlockSpec((1, tm, D), lambda b, i, j: (b, i, 0)),
                  pl.BlockSpec((1, D), lambda b, i, j: (0, 0)),
                  pl.BlockSpec((6, D), lambda b, i, j: (0, 0)),
                  pl.BlockSpec((1, 6, D), lambda b, i, j: (b, 0, 0)),
                  pl.BlockSpec((D, tn), lambda b, i, j: (0, j)),
                  pl.BlockSpec((1, tn), lambda b, i, j: (0, j))],
        out_specs=pl.BlockSpec((1, tm, tn), lambda b, i, j: (b, i, j)),
        scratch_shapes=[pltpu.VMEM((tm, D), BF16)],
        compiler_params=_cparams(("parallel", "parallel", "arbitrary")),
        name="proj",
    )(h, g.reshape(1, D), mc, ml, w, bias)


def _head_norm(o, w, b, center):
    if center:
        o = o - jnp.mean(o, axis=-1, keepdims=True)
    o = o * lax.rsqrt(jnp.mean(o * o, axis=-1, keepdims=True) + EPS)
    o = o * w
    if b is not None:
        o = o + b
    return o


def _ret_kernel(dec_ref, q_ref, k_ref, v_ref, g_ref, cos_ref, sin_ref, gw_ref, gb_ref, u_ref,
                s_scr, dm_scr, of_scr, ob_scr, *, dk, C, nctx, ntot, scale):
    h = pl.program_id(1)
    half = dk // 2
    ri = lax.broadcasted_iota(jnp.int32, (C, C), 0)
    ci = lax.broadcasted_iota(jnp.int32, (C, C), 1)
    dif = (ri - ci).astype(F32)
    pos = lax.broadcasted_iota(jnp.int32, (C, 1), 0).astype(F32)
    lg_f = jax.nn.log_sigmoid(jnp.full((1, 1), dec_ref[0, h], F32))
    lg_b = jax.nn.log_sigmoid(jnp.full((1, 1), dec_ref[1, h], F32))
    dm_scr[0] = jnp.where(ri >= ci, jnp.exp(lg_f * dif), 0.0)
    dm_scr[1] = jnp.where(ri <= ci, jnp.exp(-lg_b * dif), 0.0)
    q_dec = (jnp.exp(lg_f * (pos + 1.0)), jnp.exp(lg_b * (C - pos)))
    k_dec = (jnp.exp(lg_f * (C - 1.0 - pos)), jnp.exp(lg_b * pos))
    s_dec = (jnp.exp(lg_f * C), jnp.exp(lg_b * C))
    s_scr[...] = jnp.zeros(s_scr.shape, F32)

    def step(rows, d):
        cos = cos_ref[rows, :]
        sin = sin_ref[rows, :]
        q = q_ref[0, rows, :]
        k = k_ref[0, rows, :]
        q1, q2 = q[:, :half], q[:, half:]
        k1, k2 = k[:, :half], k[:, half:]
        q = jnp.concatenate([q1 * cos - q2 * sin, q1 * sin + q2 * cos], axis=1)
        k = jnp.concatenate([k1 * cos - k2 * sin, k1 * sin + k2 * cos], axis=1) * scale
        vb = v_ref[0, rows, :].astype(BF16)
        a = _dot_nt(q.astype(BF16), k.astype(BF16)) * dm_scr[d]
        st = s_scr[d]
        o = _dot(a.astype(BF16), vb) + _dot_nt((q * q_dec[d]).astype(BF16), st.astype(BF16))
        s_scr[d] = s_dec[d] * st + _dot_tn(vb, (k * k_dec[d]).astype(BF16))
        return o

    def body(s, carry):
        rf = pl.ds(pl.multiple_of(s * C, C), C)
        of_scr[rf, :] = step(rf, 0)
        rb = pl.ds(pl.multiple_of(_bwd_block(s, nctx, ntot) * C, C), C)
        ob_scr[rb, :] = step(rb, 1)
        return carry

    lax.fori_loop(0, ntot, body, 0)

    def fin(i, carry):
        rows = pl.ds(pl.multiple_of(i * C, C), C)
        o = _head_norm(of_scr[rows, :] + ob_scr[rows, :], gw_ref[...], gb_ref[...], True)
        g = g_ref[0, rows, :]
        u_ref[0, rows, :] = (g * jax.nn.sigmoid(g) * o).astype(BF16)
        return carry

    lax.fori_loop(0, ntot, fin, 0)


def _rope_tables(N, L, dk):
    t = np.arange(N)
    row = (t // GRID_W).astype(np.float32)
    col = (t % GRID_W).astype(np.float32)
    quarter = dk // 4
    inv = (ROPE_THETA ** (-np.arange(quarter, dtype=np.float32) / quarter)).astype(np.float32)
    ang = np.concatenate([row[:, None] * inv, col[:, None] * inv], -1).astype(np.float32)
    cos = np.concatenate([np.ones((L, dk // 2), np.float32), np.cos(ang)], 0)
    sin = np.concatenate([np.zeros((L, dk // 2), np.float32), np.sin(ang)], 0)
    return jnp.asarray(cos, F32), jnp.asarray(sin, F32)


def _retention_core(y, decay, gn_w, gn_b, L):
    B, T, n4 = y.shape
    D = n4 // 4
    H = RET_HEADS
    dk = D // H
    dv = dk
    C = _seq_block(T, L)
    cos, sin = _rope_tables(T - L, L, dk)
    blk = lambda seg: pl.BlockSpec((1, T, dk), lambda b, h, seg=seg: (b, 0, seg * H + h))
    return pl.pallas_call(
        functools.partial(_ret_kernel, dk=dk, C=C, nctx=L // C, ntot=T // C, scale=dk ** -0.5),
        out_shape=jax.ShapeDtypeStruct((B, T, D), BF16),
        grid=(B, H),
        in_specs=[pl.BlockSpec(memory_space=pltpu.SMEM),
                  blk(0), blk(1), blk(2), blk(3),
                  pl.BlockSpec((T, dk // 2), lambda b, h: (0, 0)),
                  pl.BlockSpec((T, dk // 2), lambda b, h: (0, 0)),
                  pl.BlockSpec((1, dv), lambda b, h: (0, h)),
                  pl.BlockSpec((1, dv), lambda b, h: (0, h))],
        out_specs=pl.BlockSpec((1, T, dv), lambda b, h: (b, 0, h)),
        scratch_shapes=[pltpu.VMEM((2, dv, dk), F32), pltpu.VMEM((2, C, C), F32),
                        pltpu.VMEM((T, dv), F32), pltpu.VMEM((T, dv), F32)],
        compiler_params=_cparams(("parallel", "parallel")),
        name="retention_core",
    )(decay.astype(F32), y, y, y, y, cos, sin, gn_w.reshape(1, D), gn_b.reshape(1, D))


def _chunk_cumsum(g, C, reverse):
    n = g.shape[0]
    pos = lax.broadcasted_iota(jnp.int32, (n, 1), 0) & (C - 1)
    x = g
    s = 1
    while s < C:
        if reverse:
            x = x + jnp.where(pos < C - s, pltpu.roll(x, n - s, axis=0), 0.0)
        else:
            x = x + jnp.where(pos >= s, pltpu.roll(x, s, axis=0), 0.0)
        s *= 2
    return x


def _hg_kernel(q_ref, i_ref, zf_ref, zb_ref, g_ref, lb_ref, nw_ref, u_ref,
               s_scr, of_scr, ob_scr, *, hp, dk, dv, C, CB, nctx, ntot, scale):
    nb = CB // C
    sh = C.bit_length() - 1
    ri = lax.broadcasted_iota(jnp.int32, (CB, CB), 0)
    ci = lax.broadcasted_iota(jnp.int32, (CB, CB), 1)
    same = (ri >> sh) == (ci >> sh)
    m_lo = same & (ri >= ci)
    m_up = same & (ri <= ci)
    rblk = lax.broadcasted_iota(jnp.int32, (CB, 1), 0) >> sh
    s_scr[...] = jnp.zeros(s_scr.shape, F32)

    def block(rows, hh, z_ref, mask, reverse, o_scr, idx):
        cs = slice(hh * dk, (hh + 1) * dk)
        vs = slice(hh * dv, (hh + 1) * dv)
        lb = lb_ref[:, cs]
        qr = q_ref[0, rows, cs]
        q = qr * jax.nn.sigmoid(qr) * scale
        z = z_ref[0, rows, cs]
        f = lb + (1.0 - lb) * jax.nn.sigmoid(z)
        k = (1.0 - lb) * jax.nn.sigmoid(-z)
        v = i_ref[0, rows, vs]
        b = _chunk_cumsum(jnp.log(f), C, reverse)
        b3 = b.reshape(nb, C, dk)
        e = 0 if reverse else C - 1
        b_end3 = b3[:, e:e + 1, :]
        q_in = (q * jnp.exp(b)).astype(BF16)
        k_in = (k * jnp.exp(-b)).astype(BF16)
        k_end = (k.reshape(nb, C, dk) * jnp.exp(b_end3 - b3)).reshape(CB, dk).astype(BF16)
        a = jnp.where(mask, _dot_nt(q_in, k_in), 0.0)
        o = _dot(a.astype(BF16), v.astype(BF16))
        vw = jnp.concatenate([jnp.where(rblk == c, v, 0.0).astype(BF16) for c in range(nb)], axis=1)
        kvt = _dot_tn(vw, k_end)
        st = s_scr[idx]
        enter = [None] * nb
        for c in (range(nb - 1, -1, -1) if reverse else range(nb)):
            enter[c] = st.astype(BF16)
            st = jnp.exp(b_end3[c]) * st + kvt[c * dv:(c + 1) * dv, :]
        s_scr[idx] = st
        ow = _dot_nt(q_in, jnp.concatenate(enter, axis=0))
        for c in range(nb):
            o = o + jnp.where(rblk == c, ow[:, c * dv:(c + 1) * dv], 0.0)
        o_scr[rows, vs] = o

    def body(s, carry):
        rf = pl.ds(pl.multiple_of(s * CB, CB), CB)
        rb = pl.ds(pl.multiple_of(_bwd_block(s, nctx, ntot) * CB, CB), CB)
        for hh in range(hp):
            block(rf, hh, zf_ref, m_lo, False, of_scr, 2 * hh)
            block(rb, hh, zb_ref, m_up, True, ob_scr, 2 * hh + 1)
        return carry

    lax.fori_loop(0, ntot, body, 0)

    def fin(i, carry):
        rows = pl.ds(pl.multiple_of(i * CB, CB), CB)
        for hh in range(hp):
            cs = slice(hh * dv, (hh + 1) * dv)
            o = _head_norm(of_scr[rows, cs] + ob_scr[rows, cs], nw_ref[:, cs], None, False)
            g = g_ref[0, rows, cs]
            u_ref[0, rows, cs] = (g * jax.nn.sigmoid(g) * o).astype(BF16)
        return carry

    lax.fori_loop(0, ntot, fin, 0)


def _hgrn2_core(y, lb, norm_w, L):
    B, T, n5 = y.shape
    D = n5 // 5
    H = HG_HEADS
    dk = D // H
    dv = dk
    hp = 2 if H % 2 == 0 else 1
    G = H // hp
    CB = _seq_block(T, L)
    C = min(CHUNK, CB)
    blk = lambda seg: pl.BlockSpec((1, T, hp * dk), lambda b, h, seg=seg: (b, 0, seg * G + h))
    return pl.pallas_call(
        functools.partial(_hg_kernel, hp=hp, dk=dk, dv=dv, C=C, CB=CB, nctx=L // CB, ntot=T // CB, scale=dk ** -0.5),
        out_shape=jax.ShapeDtypeStruct((B, T, D), BF16),
        grid=(B, G),
        in_specs=[blk(0), blk(1), blk(2), blk(3), blk(4),
                  pl.BlockSpec((1, hp * dk), lambda b, h: (0, h)),
                  pl.BlockSpec((1, hp * dv), lambda b, h: (0, h))],
        out_specs=pl.BlockSpec((1, T, hp * dv), lambda b, h: (b, 0, h)),
        scratch_shapes=[pltpu.VMEM((2 * hp, dv, dk), F32), pltpu.VMEM((T, hp * dv), F32),
                        pltpu.VMEM((T, hp * dv), F32)],
        compiler_params=_cparams(("parallel", "parallel")),
        name="hgrn2_core",
    )(y, y, y, y, y, lb.reshape(1, D).astype(F32), norm_w.reshape(1, D))


def _tri(C):
    r = lax.broadcasted_iota(jnp.int32, (C, C), 0)
    c = lax.broadcasted_iota(jnp.int32, (C, C), 1)
    return r >= c, r <= c


def _ml_kernel(q_ref, k_ref, v_ref, og_ref, gt_ref, nw_ref, u_ref, c_scr, n_scr, m_scr, of_scr, ob_scr,
               *, H, C, nctx, ntot, scale):
    h = pl.program_id(1)
    lower, upper = _tri(C)
    lo_b, up_b = lower.astype(BF16), upper.astype(BF16)
    c_scr[...] = jnp.zeros(c_scr.shape, F32)
    n_scr[...] = jnp.zeros(n_scr.shape, F32)
    m_scr[...] = jnp.zeros(m_scr.shape, F32)
    gw = gt_ref.shape[2]
    lane = lax.broadcasted_iota(jnp.int32, (C, gw), 1)
    eye8 = (lax.broadcasted_iota(jnp.int32, (8, gw), 0) == lax.broadcasted_iota(jnp.int32, (8, gw), 1)).astype(BF16)

    def chunk(rows, d, tri_b, mask, end_row, o_scr):
        gt = gt_ref[0, rows, :]
        cum = _dot_exact01(tri_b, jax.nn.log_sigmoid(gt))
        b_col = jnp.sum(jnp.where(lane == h + H + 2 * H * d, cum, 0.0), axis=1, keepdims=True)
        i_col = jnp.sum(jnp.where(lane == h + 2 * H * d, gt, 0.0), axis=1, keepdims=True)
        z = jnp.where(lane == 0, b_col, jnp.where(lane == 1, i_col, 0.0))
        r = _dot_nt_exact01(eye8, z)
        b_row, i_row = r[0:1, :], r[1:2, :]
        q = q_ref[0, rows, :]
        k = k_ref[0, rows, :] * scale
        v = v_ref[0, rows, :]
        m_prev = m_scr[d, 0:1, 0:1]
        logw = jnp.where(mask, b_col - b_row + i_row, -jnp.inf)
        log_carry = b_col + m_prev
        m_i = jnp.maximum(log_carry, jnp.max(logw, axis=1, keepdims=True))
        qb, vb = q.astype(BF16), v.astype(BF16)
        s = _dot_nt(qb, k.astype(BF16)) * jnp.exp(logw - m_i)
        a = jnp.exp(log_carry - m_i)
        cst = c_scr[d]
        nst = n_scr[d]
        num = _dot(s.astype(BF16), vb) + a * _dot(qb, cst.astype(BF16))
        den = jnp.sum(s, axis=1, keepdims=True) + a * jnp.sum(q * nst, axis=1, keepdims=True)
        o_scr[rows, :] = num / jnp.maximum(jnp.abs(den), jnp.exp(-m_i))
        b_end = b_col[end_row:end_row + 1, :]
        logw_end = b_end - b_col + i_col
        m_new = jnp.maximum(b_end + m_prev, jnp.max(logw_end, axis=0, keepdims=True))
        decay = jnp.exp(b_end + m_prev - m_new)
        kw = k * jnp.exp(logw_end - m_new)
        c_scr[d] = decay * cst + _dot_tn(kw.astype(BF16), vb)
        n_scr[d] = decay * nst + jnp.sum(kw, axis=0, keepdims=True)
        m_scr[d] = jnp.broadcast_to(m_new, m_scr.shape[1:])

    def body(s, carry):
        rf = pl.ds(pl.multiple_of(s * C, C), C)
        rb = pl.ds(pl.multiple_of(_bwd_block(s, nctx, ntot) * C, C), C)
        chunk(rf, 0, lo_b, lower, C - 1, of_scr)
        chunk(rb, 1, up_b, upper, 0, ob_scr)
        return carry

    lax.fori_loop(0, ntot, body, 0)

    def fin(i, carry):
        rows = pl.ds(pl.multiple_of(i * C, C), C)
        o = _head_norm(of_scr[rows, :] + ob_scr[rows, :], nw_ref[...], None, True)
        u_ref[0, rows, :] = (jax.nn.sigmoid(og_ref[0, rows, :]) * o).astype(BF16)
        return carry

    lax.fori_loop(0, ntot, fin, 0)


def _mlstm_core(y, norm_w, D, L):
    B, T, n = y.shape
    H = ML_HEADS
    dv = D // H
    dk = dv // 2
    gw = n - 3 * D
    C = _seq_block(T, L)
    return pl.pallas_call(
        functools.partial(_ml_kernel, H=H, C=C, nctx=L // C, ntot=T // C, scale=dk ** -0.5),
        out_shape=jax.ShapeDtypeStruct((B, T, D), BF16),
        grid=(B, H),
        in_specs=[pl.BlockSpec((1, T, dk), lambda b, h: (b, 0, h)),
                  pl.BlockSpec((1, T, dk), lambda b, h: (b, 0, H + h)),
                  pl.BlockSpec((1, T, dv), lambda b, h: (b, 0, H + h)),
                  pl.BlockSpec((1, T, dv), lambda b, h: (b, 0, 2 * H + h)),
                  pl.BlockSpec((1, T, gw), lambda b, h: (b, 0, 3 * D // gw)),
                  pl.BlockSpec((1, dv), lambda b, h: (0, h))],
        out_specs=pl.BlockSpec((1, T, dv), lambda b, h: (b, 0, h)),
        scratch_shapes=[pltpu.VMEM((2, dk, dv), F32), pltpu.VMEM((2, 1, dk), F32), pltpu.VMEM((2, 8, LANES_V7X), F32),
                        pltpu.VMEM((T, dv), F32), pltpu.VMEM((T, dv), F32)],
        compiler_params=_cparams(("parallel", "parallel")),
        name="mlstm_core",
    )(y, y, y, y, y, norm_w.reshape(1, D))


def _na_kernel(cls_ref, q_ref, k_ref, v_ref, bias_ref, o_ref, *, L, rows, wr, G, kr, scale):
    W = GRID_W
    kc = k_ref[0, 0:L, :].astype(BF16)
    vc = v_ref[0, 0:L, :].astype(BF16)
    qc = q_ref[0, 0:L, :].astype(BF16)
    s = _dot_nt(qc, kc) * scale
    p = jnp.exp(s - jnp.max(s, axis=1, keepdims=True))
    o_ref[0, 0:L, :] = (_dot(p.astype(BF16), vc) / jnp.sum(p, axis=1, keepdims=True)).astype(BF16)

    def body(g, carry):
        base = jnp.clip(g * G - wr // 2, 0, rows - kr)
        qrows = pl.ds(pl.multiple_of(L + g * (G * W), W), G * W)
        krows = pl.ds(pl.multiple_of(L + base * W, W), kr * W)
        q = q_ref[0, qrows, :].astype(BF16)
        kw = k_ref[0, krows, :].astype(BF16)
        vw = v_ref[0, krows, :].astype(BF16)
        s_win = _dot_nt(q, kw) * scale + bias_ref[0, cls_ref[g]]
        s_ctx = _dot_nt(q, kc) * scale
        m = jnp.maximum(jnp.max(s_win, axis=1, keepdims=True), jnp.max(s_ctx, axis=1, keepdims=True))
        p_win = jnp.exp(s_win - m)
        p_ctx = jnp.exp(s_ctx - m)
        den = jnp.sum(p_win, axis=1, keepdims=True) + jnp.sum(p_ctx, axis=1, keepdims=True)
        o = _dot(p_win.astype(BF16), vw) + _dot(p_ctx.astype(BF16), vc)
        o_ref[0, qrows, :] = (o / den).astype(BF16)
        return carry

    lax.fori_loop(0, rows // G, body, 0)


def _na_groups(rows):
    wr = min(NA_WIN_R, rows)
    G = 4 if rows % 4 == 0 else 1
    kr = min(rows, wr + G - 1)
    layouts, cls = [], []
    for g in range(rows // G):
        base = int(np.clip(g * G - wr // 2, 0, rows - kr))
        lay = []
        for rr in range(G):
            r = g * G + rr
            r0 = int(np.clip(r - wr // 2, 0, rows - wr))
            lay.append(tuple((base + kk - r + NA_WIN_R - 1) if r0 <= base + kk < r0 + wr else -1 for kk in range(kr)))
        lay = tuple(lay)
        if lay not in layouts:
            layouts.append(lay)
        cls.append(layouts.index(lay))
    return G, kr, wr, layouts, np.asarray(cls, np.int32)


def _na_bias_table(rpb, rows):
    H = rpb.shape[0]
    W = GRID_W
    G, kr, wr, layouts, cls = _na_groups(rows)
    pad = W - NA_WIN_C
    rp = rpb.astype(F32)
    wide = jnp.concatenate([jnp.repeat(rp[..., :1], pad, axis=-1), rp, jnp.repeat(rp[..., -1:], pad, axis=-1)], -1)
    toe = jnp.stack([wide[..., W - 1 - q:2 * W - 1 - q] for q in range(W)], axis=-2)
    qi = np.arange(W)
    c0 = np.clip(qi - NA_WIN_C // 2, 0, W - NA_WIN_C)
    kcol = np.arange(W)
    ok = (kcol[None, :] >= c0[:, None]) & (kcol[None, :] < c0[:, None] + NA_WIN_C)
    toe = jnp.where(jnp.asarray(ok), toe, NEG)
    masked = jnp.full((H, W, W), NEG, F32)
    tabs = []
    for lay in layouts:
        tabs.append(jnp.concatenate(
            [jnp.concatenate([toe[:, d] if d >= 0 else masked for d in row], axis=-1) for row in lay], axis=-2))
    return jnp.stack(tabs, axis=1), jnp.asarray(cls)


def _na_core(y, rpb, L):
    B, T, n3 = y.shape
    D = n3 // 3
    H = NA_HEADS
    dh = D // H
    rows = (T - L) // GRID_W
    G, kr, wr, layouts, _ = _na_groups(rows)
    bias, cls = _na_bias_table(rpb, rows)
    blk = lambda seg: pl.BlockSpec((1, T, dh), lambda b, h, seg=seg: (b, 0, seg * H + h))
    return pl.pallas_call(
        functools.partial(_na_kernel, L=L, rows=rows, wr=wr, G=G, kr=kr, scale=dh ** -0.5),
        out_shape=jax.ShapeDtypeStruct((B, T, D), BF16),
        grid=(B, H),
        in_specs=[pl.BlockSpec(memory_space=pltpu.SMEM), blk(0), blk(1), blk(2),
                  pl.BlockSpec((1, len(layouts), G * GRID_W, kr * GRID_W), lambda b, h: (h, 0, 0, 0))],
        out_specs=pl.BlockSpec((1, T, dh), lambda b, h: (b, 0, h)),
        compiler_params=_cparams(("parallel", "parallel")),
        name="na_core",
    )(cls, y, y, y, bias)


def _outproj_kernel(u_ref, wo_ref, h_ref, g_ref, mc_ref, ml_ref, wr_ref, h1_ref, f_ref, lg_ref, *, L, tm, nsub):
    E = lg_ref.shape[2]
    ts = tm // nsub
    for sub in range(nsub):
        rs = slice(sub * ts, (sub + 1) * ts)
        row0 = pl.program_id(1) * tm + sub * ts
        g1 = _mod_rows(mc_ref, ml_ref, row0, ts, L, 2)
        sh2 = _mod_rows(mc_ref, ml_ref, row0, ts, L, 3)
        sc2 = _mod_rows(mc_ref, ml_ref, row0, ts, L, 4)
        h1 = h_ref[0, rs, :] + g1 * _dot(u_ref[0, rs, :], wo_ref[...])
        h1_ref[0, rs, :] = h1
        f = _norm_mod(h1, g_ref[...], sh2, sc2)
        f_hi = f.astype(BF16)
        f_ref[0, rs, :] = f_hi
        r1 = _dot(f_hi, wr_ref[...])
        r2 = _dot((f - f_hi.astype(F32)).astype(BF16), wr_ref[...])
        lg_ref[0, rs, :] = r1[:, :E] + r1[:, E:] + r2[:, :E]


def _outproj(u, wo, h, g, mc, ml, w_router, L):
    B, T, D = h.shape
    E = w_router.shape[1]
    wr_hi = w_router.astype(BF16)
    wr_lo = (w_router - wr_hi.astype(F32)).astype(BF16)
    wr2 = jnp.concatenate([wr_hi, wr_lo], axis=1)
    tm = _pick(T, (384, 256, 128, 64))
    nsub = 2 if tm % 32 == 0 else 1
    return pl.pallas_call(
        functools.partial(_outproj_kernel, L=L, tm=tm, nsub=nsub),
        out_shape=(jax.ShapeDtypeStruct((B, T, D), F32), jax.ShapeDtypeStruct((B, T, D), BF16),
                   jax.ShapeDtypeStruct((B, T, E), F32)),
        grid=(B, T // tm),
        in_specs=[pl.BlockSpec((1, tm, D), lambda b, i: (b, i, 0)),
                  pl.BlockSpec((D, D), lambda b, i: (0, 0)),
                  pl.BlockSpec((1, tm, D), lambda b, i: (b, i, 0)),
                  pl.BlockSpec((1, D), lambda b, i: (0, 0)),
                  pl.BlockSpec((6, D), lambda b, i: (0, 0)),
                  pl.BlockSpec((1, 6, D), lambda b, i: (b, 0, 0)),
                  pl.BlockSpec((D, 2 * E), lambda b, i: (0, 0))],
        out_specs=(pl.BlockSpec((1, tm, D), lambda b, i: (b, i, 0)),
                   pl.BlockSpec((1, tm, D), lambda b, i: (b, i, 0)),
                   pl.BlockSpec((1, tm, E), lambda b, i: (b, i, 0))),
        compiler_params=_cparams(("parallel", "parallel")),
        name="outproj",
    )(u, wo, h, g.reshape(1, D), mc, ml, wr2)


def _route_kernel(lg_ref, slot_ref, aff_ref, before_scr, *, L, caps):
    x = lg_ref[...]
    B, E, T = x.shape
    e = jnp.exp(x - jnp.max(x, axis=1, keepdims=True))
    aff = e / jnp.sum(e, axis=1, keepdims=True)
    aff_ref[...] = aff
    nmax = before_scr.shape[0]
    rb = _pick(nmax, (256, 128, 64))

    def fill(i, carry):
        r = i * rb + lax.broadcasted_iota(jnp.int32, (rb, nmax), 0)
        before_scr[pl.ds(pl.multiple_of(i * rb, rb), rb), :] = jnp.where(
            r < lax.broadcasted_iota(jnp.int32, (rb, nmax), 1), 1.0, 0.0).astype(BF16)
        return carry

    lax.fori_loop(0, nmax // rb, fill, 0)
    base = 0
    for lo, hi, cap in ((L, T, caps[1]), (0, L, caps[0])):
        n = hi - lo
        before = before_scr[0:n, 0:n]
        a = aff[:, :, lo:hi].reshape(B * E, n)
        bits = lax.bitcast_convert_type(a, jnp.int32)

        def search(it, thr, bits=bits, cap=cap):
            cand = thr | lax.shift_left(jnp.int32(1), 30 - it)
            cnt = jnp.sum(jnp.where(bits >= cand, 1.0, 0.0), axis=1, keepdims=True)
            return jnp.where(cnt >= cap, cand, thr)

        thr = lax.fori_loop(0, 31, search, jnp.zeros((B * E, 1), jnp.int32))
        gt = bits > thr
        eq = bits == thr
        need = cap - jnp.sum(jnp.where(gt, 1.0, 0.0), axis=1, keepdims=True)
        rank_eq = _dot(jnp.where(eq, 1.0, 0.0).astype(BF16), before)
        sel = gt | (eq & (rank_eq < need))
        pos = _dot(jnp.where(sel, 1.0, 0.0).astype(BF16), before)
        slot = jnp.where(sel, pos.astype(jnp.int32) + base, -1)
        slot_ref[:, :, lo:hi] = slot.reshape(B, E, n)
        base += cap


def _route(logits_t, L, caps):
    B, E, T = logits_t.shape
    return pl.pallas_call(
        functools.partial(_route_kernel, L=L, caps=caps),
        out_shape=(jax.ShapeDtypeStruct((B, E, T), jnp.int32), jax.ShapeDtypeStruct((B, E, T), F32)),
        scratch_shapes=[pltpu.VMEM((max(L, T - L), max(L, T - L)), BF16)],
        compiler_params=pltpu.CompilerParams(vmem_limit_bytes=VMEM_LIMIT_BYTES_V7X),
        name="route",
    )(logits_t)


def _gather_kernel(slot_ref, aff_ref, f_ref, x_ref, gate_ref, *, R):
    T = f_ref.shape[1]
    hit = lax.broadcasted_iota(jnp.int32, (R, T), 0) == slot_ref[0, 0]
    x_ref[0, 0] = _dot(jnp.where(hit, 1.0, 0.0).astype(BF16), f_ref[0]).astype(BF16)
    gate = jnp.sum(jnp.where(hit, aff_ref[0, 0], 0.0), axis=1, keepdims=True)
    gate_ref[0, 0] = jnp.broadcast_to(gate, gate_ref.shape[2:])


def _gather(slot, aff, f, R):
    B, E, T = slot.shape
    D = f.shape[2]
    return pl.pallas_call(
        functools.partial(_gather_kernel, R=R),
        out_shape=(jax.ShapeDtypeStruct((B, E, R, D), BF16), jax.ShapeDtypeStruct((B, E, R, LANES_V7X), F32)),
        grid=(B, E),
        in_specs=[pl.BlockSpec((1, 1, 1, T), lambda b, e: (b, e, 0, 0)),
                  pl.BlockSpec((1, 1, 1, T), lambda b, e: (b, e, 0, 0)),
                  pl.BlockSpec((1, T, D), lambda b, e: (b, 0, 0))],
        out_specs=(pl.BlockSpec((1, 1, R, D), lambda b, e: (b, e, 0, 0)),
                   pl.BlockSpec((1, 1, R, LANES_V7X), lambda b, e: (b, e, 0, 0))),
        compiler_params=_cparams(("parallel", "arbitrary")),
        name="moe_gather",
    )(slot.reshape(B, E, 1, T), aff.reshape(B, E, 1, T), f)


def _ffn_kernel(x_ref, gate_ref, w1_ref, w3_ref, w2_ref, y_ref):
    bs, _, R, D = x_ref.shape
    x = x_ref[...].reshape(bs * R, D)
    u = _dot(x, w1_ref[0])
    g = _dot(x, w3_ref[0])
    hm = (u * jax.nn.sigmoid(u) * g).astype(BF16)
    gate = gate_ref[...].reshape(bs * R, gate_ref.shape[3])[:, 0:1]
    y_ref[...] = (_dot(hm, w2_ref[0]) * gate).astype(BF16).reshape(bs, 1, R, D)


def _ffn(xg, gate, w1, w3, w2):
    B, E, R, D = xg.shape
    FF = w1.shape[2]
    bs = _pick(B, (2, 1))
    return pl.pallas_call(
        _ffn_kernel,
        out_shape=jax.ShapeDtypeStruct((B, E, R, D), BF16),
        grid=(E, B // bs),
        in_specs=[pl.BlockSpec((bs, 1, R, D), lambda e, b: (b, e, 0, 0)),
                  pl.BlockSpec((bs, 1, R, LANES_V7X), lambda e, b: (b, e, 0, 0)),
                  pl.BlockSpec((1, D, FF), lambda e, b: (e, 0, 0)),
                  pl.BlockSpec((1, D, FF), lambda e, b: (e, 0, 0)),
                  pl.BlockSpec((1, FF, D), lambda e, b: (e, 0, 0))],
        out_specs=pl.BlockSpec((bs, 1, R, D), lambda e, b: (b, e, 0, 0)),
        compiler_params=_cparams(("parallel", "arbitrary")),
        name="moe_ffn",
    )(xg, gate, w1, w3, w2)


def _combine_kernel(slotc_ref, y_ref, h1_ref, mc_ref, ml_ref, o_ref, *, L, tt, caps):
    E = y_ref.shape[1]
    capc, capl = caps
    i = pl.program_id(2)
    row0 = i * tt
    sc = slotc_ref[0]
    g2 = _mod_rows(mc_ref, ml_ref, row0, tt, L, 5)

    @pl.when(row0 >= L)
    def _():
        jl = lax.broadcasted_iota(jnp.int32, (1, capl), 1).astype(F32)
        hit = jnp.concatenate([jnp.where(sc[:, e:e + 1] == jl, 1.0, 0.0).astype(BF16) for e in range(E)], axis=1)
        yl = y_ref[0, :, 0:capl, :].reshape(E * capl, y_ref.shape[3])
        o_ref[0] = h1_ref[0] + g2 * _dot(hit, yl)

    @pl.when(row0 < L)
    def _():
        jc = lax.broadcasted_iota(jnp.int32, (1, E * capc), 1).astype(F32)
        hit = jnp.zeros((tt, E * capc), F32)
        for e in range(E):
            se = sc[:, e:e + 1]
            hit = hit + jnp.where(jnp.where(se >= 0, se + (e * capc - capl), -1.0) == jc, 1.0, 0.0)
        yc = y_ref[0, :, capl:capl + capc, :].reshape(E * capc, y_ref.shape[3])
        o_ref[0] = h1_ref[0] + g2 * _dot(hit.astype(BF16), yc)


def _combine(slot_c, yff, h1, mc, ml, L, caps):
    B, T, D = h1.shape
    E, R = yff.shape[1], yff.shape[2]
    tt = _pick(L, (256, 128, 64))
    td = _pick(D, (1024, 512, 256, 128))
    return pl.pallas_call(
        functools.partial(_combine_kernel, L=L, tt=tt, caps=caps),
        out_shape=jax.ShapeDtypeStruct((B, T, D), F32),
        grid=(B, D // td, T // tt),
        in_specs=[pl.BlockSpec((1, tt, E), lambda b, j, i: (b, i, 0)),
                  pl.BlockSpec((1, E, R, td), lambda b, j, i: (b, 0, 0, j)),
                  pl.BlockSpec((1, tt, td), lambda b, j, i: (b, i, j)),
                  pl.BlockSpec((6, td), lambda b, j, i: (0, j)),
                  pl.BlockSpec((1, 6, td), lambda b, j, i: (b, 0, j))],
        out_specs=pl.BlockSpec((1, tt, td), lambda b, j, i: (b, i, j)),
        compiler_params=_cparams(("parallel", "parallel", "arbitrary")),
        name="moe_combine",
    )(slot_c, yff, h1, mc, ml)


def _moe(f, logits, h1, mc, ml, w1, w3, w2, L):
    B, T, D = h1.shape
    caps = tuple(max(1, EC_CAPACITY * n // N_EXPERTS) for n in (L, T - L))
    slot, aff = _route(jnp.swapaxes(logits, 1, 2), L, caps)
    xg, gate = _gather(slot, aff, f, caps[0] + caps[1])
    yff = _ffn(xg, gate, w1, w3, w2)
    slot_c = jnp.swapaxes(slot, 1, 2).astype(F32)
    return _combine(slot_c, yff, h1, mc, ml, L, caps)


def _final_kernel(h_ref, g_ref, o_ref):
    h = h_ref[0]
    o_ref[0] = h * lax.rsqrt(jnp.mean(h * h, axis=-1, keepdims=True) + EPS) * g_ref[...]


def _final_norm(h, g, L):
    B, T, D = h.shape
    tr = _pick(L, (256, 128, 64))
    off = L // tr
    return pl.pallas_call(
        _final_kernel,
        out_shape=jax.ShapeDtypeStruct((B, T - L, D), F32),
        grid=(B, (T - L) // tr),
        in_specs=[pl.BlockSpec((1, tr, D), lambda b, i: (b, i + off, 0)),
                  pl.BlockSpec((1, D), lambda b, i: (0, 0))],
        out_specs=pl.BlockSpec((1, tr, D), lambda b, i: (b, i, 0)),
        compiler_params=_cparams(("parallel", "parallel")),
        name="final_norm",
    )(h, g.reshape(1, D))


def kernel(x, c, ctx, c_ctx, ada_w, ada_b, norm_g, final_g, ret_wq, ret_wk, ret_wv, ret_wg, ret_wo, ret_decay, ret_gn_w, ret_gn_b, ml_wq, ml_wk, ml_wv, ml_wog, ml_wgate, ml_bgate, ml_norm_w, ml_wout, na_wqkv, na_rpb, na_wo, hg_wq, hg_wi, hg_wf, hg_wg, hg_norm_w, hg_wo, hg_lb, moe_router, moe_w1, moe_w3, moe_w2):
    B, N, D = x.shape
    L = ctx.shape[1]
    depth = ada_w.shape[0]
    h = jnp.concatenate([ctx, x], axis=1)
    mod_lat, mod_ctx = _mod_vectors(c, c_ctx, ada_w, ada_b)
    lb_all = jnp.cumsum(jax.nn.softmax(hg_lb.astype(F32), axis=0), axis=0)
    zero_b = lambda n: jnp.zeros((1, n), F32)
    gpad = LANES_V7X - 4 * ML_HEADS
    for i in range(depth):
        mc, ml = mod_ctx[i], mod_lat[i]
        kind = i % N_MIXERS
        if kind == 0:
            w = jnp.concatenate([ret_wq, ret_wk, ret_wv, ret_wg], 1).astype(BF16)
            y = _proj(h, norm_g[i, 0], mc, ml, w, zero_b(w.shape[1]), L)
            u = _retention_core(y, ret_decay, ret_gn_w, ret_gn_b, L)
            wo = ret_wo
        elif kind == 1:
            w = jnp.concatenate([ml_wq, ml_wk, ml_wv, ml_wog, ml_wgate[0], ml_wgate[1], jnp.zeros((D, gpad), F32)],
                                1).astype(BF16)
            bias = jnp.concatenate([jnp.zeros((3 * D,), F32), ml_bgate[0], ml_bgate[1], jnp.zeros((gpad,), F32)])[None]
            y = _proj(h, norm_g[i, 0], mc, ml, w, bias, L)
            u = _mlstm_core(y, ml_norm_w, D, L)
            wo = ml_wout
        elif kind == 2:
            w = na_wqkv.astype(BF16)
            y = _proj(h, norm_g[i, 0], mc, ml, w, zero_b(w.shape[1]), L)
            u = _na_core(y, na_rpb, L)
            wo = na_wo
        else:
            w = jnp.concatenate([hg_wq, hg_wi, hg_wf[0], hg_wf[1], hg_wg], 1).astype(BF16)
            y = _proj(h, norm_g[i, 0], mc, ml, w, zero_b(w.shape[1]), L)
            u = _hgrn2_core(y, lb_all[i] - lb_all[0], hg_norm_w, L)
            wo = hg_wo
        h1, f, logits = _outproj(u, wo.astype(BF16), h, norm_g[i, 1], mc, ml, moe_router[i], L)
        h = _moe(f, logits, h1, mc, ml, moe_w1[i].astype(BF16), moe_w3[i].astype(BF16), moe_w2[i].astype(BF16), L)
    return _final_norm(h, final_g, L)
```

```python
import functools

import numpy as np
import jax
import jax.numpy as jnp
from jax import lax
from jax.experimental import pallas as pl
from jax.experimental.pallas import tpu as pltpu

GRID_W = 64
EPS = 1e-6
NEG = -1e30
CHUNK = 64
ROPE_THETA = 10000.0
RET_HEADS = 8
ML_HEADS = 8
NA_HEADS = 16
NA_WIN_R = 8
NA_WIN_C = 16
HG_HEADS = 16
N_EXPERTS = 16
EC_CAPACITY = 2
N_MIXERS = 4

VMEM_LIMIT_BYTES_V7X = 58 * 1024 * 1024
LANES_V7X = 128

F32 = jnp.float32
BF16 = jnp.bfloat16


def _cparams(sem):
    return pltpu.CompilerParams(dimension_semantics=sem, vmem_limit_bytes=VMEM_LIMIT_BYTES_V7X)


def _pick(n, prefs):
    for p in prefs:
        if n % p == 0:
            return p
    return n


def _dot(a, b):
    return jnp.dot(a, b, preferred_element_type=F32)


def _dot_nt(a, b):
    return lax.dot_general(a, b, (((1,), (1,)), ((), ())), preferred_element_type=F32)


def _dot_tn(a, b):
    return lax.dot_general(a, b, (((0,), (0,)), ((), ())), preferred_element_type=F32)


def _norm_mod(h, g, sh, sc):
    ms = jnp.mean(h * h, axis=-1, keepdims=True)
    y = h * lax.rsqrt(ms + EPS) * g
    return y * (1.0 + sc) + sh


def _mod_rows(mc_ref, ml_ref, row0, tm, L, k):
    row = row0 + lax.broadcasted_iota(jnp.int32, (tm, 1), 0)
    return jnp.where(row < L, mc_ref[k:k + 1, :], ml_ref[0, k:k + 1, :])


def _seq_block(T, L):
    cb = _pick(L, (256, 128, 64))
    assert L % cb == 0 and (T - L) % cb == 0
    return cb


def _bwd_block(s, nctx, ntot):
    return jnp.where(s < nctx, nctx - 1 - s, ntot - 1 - (s - nctx))


def _mod_kernel(c_ref, w_ref, b_ref, o_ref):
    c = c_ref[...]
    s = (c * jax.nn.sigmoid(c)).astype(BF16)
    o_ref[0] = _dot(s, w_ref[0].astype(BF16)) + b_ref[0]


def _mod_vectors(c, c_ctx, ada_w, ada_b):
    B, D = c.shape
    depth, _, n6 = ada_w.shape
    mp = -(-(B + 1) // 8) * 8
    c_all = jnp.concatenate([c, c_ctx[None], jnp.zeros((mp - B - 1, D), F32)], 0)
    tn = _pick(n6, (1024, 512, 256, 128))
    out = pl.pallas_call(
        _mod_kernel,
        out_shape=jax.ShapeDtypeStruct((depth, mp, n6), F32),
        grid=(depth, n6 // tn),
        in_specs=[pl.BlockSpec((mp, D), lambda i, j: (0, 0)),
                  pl.BlockSpec((1, D, tn), lambda i, j: (i, 0, j)),
                  pl.BlockSpec((1, 1, tn), lambda i, j: (i, 0, j))],
        out_specs=pl.BlockSpec((1, mp, tn), lambda i, j: (i, 0, j)),
        compiler_params=_cparams(("parallel", "parallel")),
        name="mod_vectors",
    )(c_all, ada_w, ada_b.reshape(depth, 1, n6))
    mod_lat = out[:, :B].reshape(depth, B, 6, D)
    mod_ctx = out[:, B].reshape(depth, 6, D)
    return mod_lat, mod_ctx


def _proj_kernel(h_ref, g_ref, mc_ref, ml_ref, w_ref, b_ref, o_ref, a_scr, *, L, tm, ts):
    i = pl.program_id(1)
    j = pl.program_id(2)

    @pl.when(j == 0)
    def _():
        for sub in range(tm // ts):
            rs = slice(sub * ts, (sub + 1) * ts)
            sh = _mod_rows(mc_ref, ml_ref, i * tm + sub * ts, ts, L, 0)
            sc = _mod_rows(mc_ref, ml_ref, i * tm + sub * ts, ts, L, 1)
            a = _norm_mod(h_ref[0, rs, :], g_ref[...], sh, sc).astype(BF16)
            a_scr[rs, :] = a
            o_ref[0, rs, :] = (_dot(a, w_ref[...]) + b_ref[...]).astype(o_ref.dtype)

    @pl.when(j > 0)
    def _():
        o_ref[0] = (_dot(a_scr[...], w_ref[...]) + b_ref[...]).astype(o_ref.dtype)


def _proj(h, g, mc, ml, w, bias, L, out_dtype=F32):
    B, T, D = h.shape
    n = w.shape[1]
    tm = _pick(T, (768, 576, 384, 256, 128, 64))
    ts = _pick(tm, (128, 64))
    tn = _pick(n, (1024, 896, 512, 256, 128))
    return pl.pallas_call(
        functools.partial(_proj_kernel, L=L, tm=tm, ts=ts),
        out_shape=jax.ShapeDtypeStruct((B, T, n), out_dtype),
        grid=(B, T // tm, n // tn),
        in_specs=[pl.BlockSpec((1, tm, D), lambda b, i, j: (b, i, 0)),
                  pl.BlockSpec((1, D), lambda b, i, j: (0, 0)),
                  pl.BlockSpec((6, D), lambda b, i, j: (0, 0)),
                  pl.BlockSpec((1, 6, D), lambda b, i, j: (b, 0, 0)),
                  pl.BlockSpec((D, tn), lambda b, i, j: (0, j)),
                  pl.BlockSpec((1, tn), lambda b, i, j: (0, j))],
        out_specs=pl.BlockSpec((1, tm, tn), lambda b, i, j: (b, i, j)),
        scratch_shapes=[pltpu.VMEM((tm, D), BF16)],
        compiler_params=_cparams(("parallel", "parallel", "arbitrary")),
        name="proj",
    )(h, g.reshape(1, D), mc, ml, w, bias)


def _head_norm(o, w, b, center):
    if center:
        o = o - jnp.mean(o, axis=-1, keepdims=True)
    o = o * lax.rsqrt(jnp.mean(o * o, axis=-1, keepdims=True) + EPS)
    o = o * w
    if b is not None:
        o = o + b
    return o


def _ret_kernel(dec_ref, q_ref, k_ref, v_ref, g_ref, cos_ref, sin_ref, gw_ref, gb_ref, u_ref,
                s_scr, dm_scr, of_scr, ob_scr, *, dk, C, nctx, ntot, scale):
    h = pl.program_id(1)
    half = dk // 2
    ri = lax.broadcasted_iota(jnp.int32, (C, C), 0)
    ci = lax.broadcasted_iota(jnp.int32, (C, C), 1)
    dif = (ri - ci).astype(F32)
    pos = lax.broadcasted_iota(jnp.int32, (C, 1), 0).astype(F32)
    lg_f = jax.nn.log_sigmoid(jnp.full((1, 1), dec_ref[0, h], F32))
    lg_b = jax.nn.log_sigmoid(jnp.full((1, 1), dec_ref[1, h], F32))
    dm_scr[0] = jnp.where(ri >= ci, jnp.exp(lg_f * dif), 0.0)
    dm_scr[1] = jnp.where(ri <= ci, jnp.exp(-lg_b * dif), 0.0)
    q_dec = (jnp.exp(lg_f * (pos + 1.0)), jnp.exp(lg_b * (C - pos)))
    k_dec = (jnp.exp(lg_f * (C - 1.0 - pos)), jnp.exp(lg_b * pos))
    s_dec = (jnp.exp(lg_f * C), jnp.exp(lg_b * C))
    s_scr[...] = jnp.zeros(s_scr.shape, F32)

    def step(rows, d):
        cos = cos_ref[rows, :]
        sin = sin_ref[rows, :]
        q = q_ref[0, rows, :]
        k = k_ref[0, rows, :]
        q1, q2 = q[:, :half], q[:, half:]
        k1, k2 = k[:, :half], k[:, half:]
        q = jnp.concatenate([q1 * cos - q2 * sin, q1 * sin + q2 * cos], axis=1)
        k = jnp.concatenate([k1 * cos - k2 * sin, k1 * sin + k2 * cos], axis=1) * scale
        vb = v_ref[0, rows, :].astype(BF16)
        a = _dot_nt(q.astype(BF16), k.astype(BF16)) * dm_scr[d]
        st = s_scr[d]
        o = _dot(a.astype(BF16), vb) + _dot_nt((q * q_dec[d]).astype(BF16), st.astype(BF16))
        s_scr[d] = s_dec[d] * st + _dot_tn(vb, (k * k_dec[d]).astype(BF16))
        return o

    def body(s, carry):
        rf = pl.ds(pl.multiple_of(s * C, C), C)
        of_scr[rf, :] = step(rf, 0)
        rb = pl.ds(pl.multiple_of(_bwd_block(s, nctx, ntot) * C, C), C)
        ob_scr[rb, :] = step(rb, 1)
        return carry

    lax.fori_loop(0, ntot, body, 0, unroll=True)

    def fin(i, carry):
        rows = pl.ds(pl.multiple_of(i * C, C), C)
        o = _head_norm(of_scr[rows, :] + ob_scr[rows, :], gw_ref[...], gb_ref[...], True)
        g = g_ref[0, rows, :]
        u_ref[0, rows, :] = (g * jax.nn.sigmoid(g) * o).astype(BF16)
        return carry

    lax.fori_loop(0, ntot, fin, 0)


def _rope_tables(N, L, dk):
    t = np.arange(N)
    row = (t // GRID_W).astype(np.float32)
    col = (t % GRID_W).astype(np.float32)
    quarter = dk // 4
    inv = (ROPE_THETA ** (-np.arange(quarter, dtype=np.float32) / quarter)).astype(np.float32)
    ang = np.concatenate([row[:, None] * inv, col[:, None] * inv], -1).astype(np.float32)
    cos = np.concatenate([np.ones((L, dk // 2), np.float32), np.cos(ang)], 0)
    sin = np.concatenate([np.zeros((L, dk // 2), np.float32), np.sin(ang)], 0)
    return jnp.asarray(cos, F32), jnp.asarray(sin, F32)


def _retention_core(y, decay, gn_w, gn_b, L):
    B, T, n4 = y.shape
    D = n4 // 4
    H = RET_HEADS
    dk = D // H
    dv = dk
    C = _seq_block(T, L)
    cos, sin = _rope_tables(T - L, L, dk)
    blk = lambda seg: pl.BlockSpec((1, T, dk), lambda b, h, seg=seg: (b, 0, seg * H + h))
    return pl.pallas_call(
        functools.partial(_ret_kernel, dk=dk, C=C, nctx=L // C, ntot=T // C, scale=dk ** -0.5),
        out_shape=jax.ShapeDtypeStruct((B, T, D), BF16),
        grid=(B, H),
        in_specs=[pl.BlockSpec(memory_space=pltpu.SMEM),
                  blk(0), blk(1), blk(2), blk(3),
                  pl.BlockSpec((T, dk // 2), lambda b, h: (0, 0)),
                  pl.BlockSpec((T, dk // 2), lambda b, h: (0, 0)),
                  pl.BlockSpec((1, dv), lambda b, h: (0, h)),
                  pl.BlockSpec((1, dv), lambda b, h: (0, h))],
        out_specs=pl.BlockSpec((1, T, dv), lambda b, h: (b, 0, h)),
        scratch_shapes=[pltpu.VMEM((2, dv, dk), F32), pltpu.VMEM((2, C, C), F32),
                        pltpu.VMEM((T, dv), F32), pltpu.VMEM((T, dv), F32)],
        compiler_params=_cparams(("parallel", "parallel")),
        name="retention_core",
    )(decay.astype(F32), y, y, y, y, cos, sin, gn_w.reshape(1, D), gn_b.reshape(1, D))


def _chunk_cumsum(g, C, reverse):
    n = g.shape[0]
    pos = lax.broadcasted_iota(jnp.int32, (n, 1), 0) & (C - 1)
    x = g
    s = 1
    while s < C:
        if reverse:
            x = x + jnp.where(pos < C - s, pltpu.roll(x, n - s, axis=0), 0.0)
        else:
            x = x + jnp.where(pos >= s, pltpu.roll(x, s, axis=0), 0.0)
        s *= 2
    return x


def _hg_kernel(q_ref, i_ref, zf_ref, zb_ref, g_ref, lb_ref, nw_ref, u_ref,
               s_scr, of_scr, ob_scr, *, hp, dk, dv, C, CB, nctx, ntot, scale):
    nb = CB // C
    sh = C.bit_length() - 1
    ri = lax.broadcasted_iota(jnp.int32, (CB, CB), 0)
    ci = lax.broadcasted_iota(jnp.int32, (CB, CB), 1)
    same = (ri >> sh) == (ci >> sh)
    m_lo = same & (ri >= ci)
    m_up = same & (ri <= ci)
    s_scr[...] = jnp.zeros(s_scr.shape, F32)

    def block(rows, hh, z_ref, mask, reverse, o_scr, idx):
        cs = slice(hh * dk, (hh + 1) * dk)
        vs = slice(hh * dv, (hh + 1) * dv)
        lb = lb_ref[:, cs]
        qr = q_ref[0, rows, cs]
        q = qr * jax.nn.sigmoid(qr) * scale
        z = z_ref[0, rows, cs]
        sig = jax.nn.sigmoid(z)
        f = lb + (1.0 - lb) * sig
        k = (1.0 - lb) * (1.0 - sig)
        vb = i_ref[0, rows, vs].astype(BF16)
        b = _chunk_cumsum(jnp.log(f), C, reverse)
        b3 = b.reshape(nb, C, dk)
        e = 0 if reverse else C - 1
        b_end3 = b3[:, e:e + 1, :]
        q_in = (q * jnp.exp(b)).astype(BF16)
        k_in = (k * jnp.exp(-b)).astype(BF16)
        k_end = (k.reshape(nb, C, dk) * jnp.exp(b_end3 - b3)).reshape(CB, dk).astype(BF16)
        a = jnp.where(mask, _dot_nt(q_in, k_in), 0.0)
        o = _dot(a.astype(BF16), vb)
        st = s_scr[idx]
        inter = [None] * nb
        for c in (range(nb - 1, -1, -1) if reverse else range(nb)):
            rc = slice(c * C, (c + 1) * C)
            inter[c] = _dot_nt(q_in[rc], st.astype(BF16))
            st = jnp.exp(b_end3[c]) * st + _dot_tn(vb[rc], k_end[rc])
        s_scr[idx] = st
        o_scr[rows, vs] = o + jnp.concatenate(inter, axis=0)

    def body(s, carry):
        rf = pl.ds(pl.multiple_of(s * CB, CB), CB)
        rb = pl.ds(pl.multiple_of(_bwd_block(s, nctx, ntot) * CB, CB), CB)
        for hh in range(hp):
            block(rf, hh, zf_ref, m_lo, False, of_scr, 2 * hh)
            block(rb, hh, zb_ref, m_up, True, ob_scr, 2 * hh + 1)
        return carry

    lax.fori_loop(0, ntot, body, 0)

    def fin(i, carry):
        rows = pl.ds(pl.multiple_of(i * CB, CB), CB)
        for hh in range(hp):
            cs = slice(hh * dv, (hh + 1) * dv)
            o = _head_norm(of_scr[rows, cs] + ob_scr[rows, cs], nw_ref[:, cs], None, False)
            g = g_ref[0, rows, cs]
            u_ref[0, rows, cs] = (g * jax.nn.sigmoid(g) * o).astype(BF16)
        return carry

    lax.fori_loop(0, ntot, fin, 0)


def _hgrn2_core(y, lb, norm_w, L):
    B, T, n5 = y.shape
    D = n5 // 5
    H = HG_HEADS
    dk = D // H
    dv = dk
    hp = 2 if H % 2 == 0 else 1
    G = H // hp
    CB = _seq_block(T, L)
    C = min(CHUNK, CB)
    blk = lambda seg: pl.BlockSpec((1, T, hp * dk), lambda b, h, seg=seg: (b, 0, seg * G + h))
    return pl.pallas_call(
        functools.partial(_hg_kernel, hp=hp, dk=dk, dv=dv, C=C, CB=CB, nctx=L // CB, ntot=T // CB, scale=dk ** -0.5),
        out_shape=jax.ShapeDtypeStruct((B, T, D), BF16),
        grid=(B, G),
        in_specs=[blk(0), blk(1), blk(2), blk(3), blk(4),
                  pl.BlockSpec((1, hp * dk), lambda b, h: (0, h)),
                  pl.BlockSpec((1, hp * dv), lambda b, h: (0, h))],
        out_specs=pl.BlockSpec((1, T, hp * dv), lambda b, h: (b, 0, h)),
        scratch_shapes=[pltpu.VMEM((2 * hp, dv, dk), F32), pltpu.VMEM((T, hp * dv), F32),
                        pltpu.VMEM((T, hp * dv), F32)],
        compiler_params=_cparams(("parallel", "parallel")),
        name="hgrn2_core",
    )(y, y, y, y, y, lb.reshape(1, D).astype(F32), norm_w.reshape(1, D))


def _tri(C):
    r = lax.broadcasted_iota(jnp.int32, (C, C), 0)
    c = lax.broadcasted_iota(jnp.int32, (C, C), 1)
    return r >= c, r <= c


def _ml_kernel(q_ref, k_ref, v_ref, og_ref, gt_ref, nw_ref, u_ref, c_scr, n_scr, m_scr, of_scr, ob_scr,
               *, H, C, nctx, ntot, scale):
    h = pl.program_id(1)
    lower, upper = _tri(C)
    c_scr[...] = jnp.zeros(c_scr.shape, F32)
    n_scr[...] = jnp.zeros(n_scr.shape, F32)
    m_scr[...] = jnp.zeros(m_scr.shape, F32)
    gw = gt_ref.shape[2]
    lane = lax.broadcasted_iota(jnp.int32, (C, gw), 1)

    def chunk(rows, d, mask, end_row, o_scr):
        gt = gt_ref[0, rows, :]
        cum = _chunk_cumsum(jax.nn.log_sigmoid(gt), C, d == 1)
        b_col = jnp.sum(jnp.where(lane == h + H + 2 * H * d, cum, 0.0), axis=1, keepdims=True)
        i_col = jnp.sum(jnp.where(lane == h + 2 * H * d, gt, 0.0), axis=1, keepdims=True)
        z = jnp.where(lane == 0, b_col, jnp.where(lane == 1, i_col, 0.0))
        r = jnp.transpose(z)
        b_row, i_row = r[0:1, :], r[1:2, :]
        q = q_ref[0, rows, :]
        k = k_ref[0, rows, :] * scale
        v = v_ref[0, rows, :]
        m_prev = m_scr[d, 0:1, 0:1]
        logw = jnp.where(mask, b_col - b_row + i_row, -jnp.inf)
        log_carry = b_col + m_prev
        m_i = jnp.maximum(log_carry, jnp.max(logw, axis=1, keepdims=True))
        qb, vb = q.astype(BF16), v.astype(BF16)
        s = _dot_nt(qb, k.astype(BF16)) * jnp.exp(logw - m_i)
        a = jnp.exp(log_carry - m_i)
        cst = c_scr[d]
        nst = n_scr[d]
        num = _dot(s.astype(BF16), vb) + a * _dot(qb, cst.astype(BF16))
        den = jnp.sum(s, axis=1, keepdims=True) + a * jnp.sum(q * nst, axis=1, keepdims=True)
        o_scr[rows, :] = num / jnp.maximum(jnp.abs(den), jnp.exp(-m_i))
        b_end = b_col[end_row:end_row + 1, :]
        logw_end = b_end - b_col + i_col
        m_new = jnp.maximum(b_end + m_prev, jnp.max(logw_end, axis=0, keepdims=True))
        decay = jnp.exp(b_end + m_prev - m_new)
        kw = k * jnp.exp(logw_end - m_new)
        c_scr[d] = decay * cst + _dot_tn(kw.astype(BF16), vb)
        n_scr[d] = decay * nst + jnp.sum(kw, axis=0, keepdims=True)
        m_scr[d] = jnp.broadcast_to(m_new, m_scr.shape[1:])

    def body(s, carry):
        rf = pl.ds(pl.multiple_of(s * C, C), C)
        rb = pl.ds(pl.multiple_of(_bwd_block(s, nctx, ntot) * C, C), C)
        chunk(rf, 0, lower, C - 1, of_scr)
        chunk(rb, 1, upper, 0, ob_scr)
        return carry

    lax.fori_loop(0, ntot, body, 0, unroll=True)

    def fin(i, carry):
        rows = pl.ds(pl.multiple_of(i * C, C), C)
        o = _head_norm(of_scr[rows, :] + ob_scr[rows, :], nw_ref[...], None, True)
        u_ref[0, rows, :] = (jax.nn.sigmoid(og_ref[0, rows, :]) * o).astype(BF16)
        return carry

    lax.fori_loop(0, ntot, fin, 0)


def _mlstm_core(y, norm_w, D, L):
    B, T, n = y.shape
    H = ML_HEADS
    dv = D // H
    dk = dv // 2
    gw = n - 3 * D
    C = _seq_block(T, L)
    return pl.pallas_call(
        functools.partial(_ml_kernel, H=H, C=C, nctx=L // C, ntot=T // C, scale=dk ** -0.5),
        out_shape=jax.ShapeDtypeStruct((B, T, D), BF16),
        grid=(B, H),
        in_specs=[pl.BlockSpec((1, T, dk), lambda b, h: (b, 0, h)),
                  pl.BlockSpec((1, T, dk), lambda b, h: (b, 0, H + h)),
                  pl.BlockSpec((1, T, dv), lambda b, h: (b, 0, H + h)),
                  pl.BlockSpec((1, T, dv), lambda b, h: (b, 0, 2 * H + h)),
                  pl.BlockSpec((1, T, gw), lambda b, h: (b, 0, 3 * D // gw)),
                  pl.BlockSpec((1, dv), lambda b, h: (0, h))],
        out_specs=pl.BlockSpec((1, T, dv), lambda b, h: (b, 0, h)),
        scratch_shapes=[pltpu.VMEM((2, dk, dv), F32), pltpu.VMEM((2, 1, dk), F32), pltpu.VMEM((2, 8, LANES_V7X), F32),
                        pltpu.VMEM((T, dv), F32), pltpu.VMEM((T, dv), F32)],
        compiler_params=_cparams(("parallel", "parallel")),
        name="mlstm_core",
    )(y, y, y, y, y, norm_w.reshape(1, D))


def _na_kernel(q_ref, k_ref, v_ref, toe_ref, o_ref, bias_scr, *, L, rows, wr, G, kr, layouts, cls, scale):
    W = GRID_W

    @pl.when(pl.program_id(1) == 0)
    def _():
        for l, lay in enumerate(layouts):
            for rr, row in enumerate(lay):
                for kk, d in enumerate(row):
                    blk = toe_ref[0, d] if d >= 0 else jnp.full((W, W), NEG, F32)
                    bias_scr[l, rr * W:(rr + 1) * W, kk * W:(kk + 1) * W] = blk

    kc = k_ref[0, 0:L, :].astype(BF16)
    vc = v_ref[0, 0:L, :].astype(BF16)
    qc = q_ref[0, 0:L, :].astype(BF16)
    s = _dot_nt(qc, kc) * scale
    p = jnp.exp(s - jnp.max(s, axis=1, keepdims=True))
    o_ref[0, 0:L, :] = (_dot(p.astype(BF16), vc) / jnp.sum(p, axis=1, keepdims=True)).astype(BF16)

    for g in range(rows // G):
        base = min(max(g * G - wr // 2, 0), rows - kr)
        qrows = slice(L + g * G * W, L + (g + 1) * G * W)
        krows = slice(L + base * W, L + (base + kr) * W)
        q = q_ref[0, qrows, :].astype(BF16)
        kw = k_ref[0, krows, :].astype(BF16)
        vw = v_ref[0, krows, :].astype(BF16)
        s_win = _dot_nt(q, kw) * scale + bias_scr[cls[g]]
        s_ctx = _dot_nt(q, kc) * scale
        m = jnp.maximum(jnp.max(s_win, axis=1, keepdims=True), jnp.max(s_ctx, axis=1, keepdims=True))
        p_win = jnp.exp(s_win - m)
        p_ctx = jnp.exp(s_ctx - m)
        den = jnp.sum(p_win, axis=1, keepdims=True) + jnp.sum(p_ctx, axis=1, keepdims=True)
        o = _dot(p_win.astype(BF16), vw) + _dot(p_ctx.astype(BF16), vc)
        o_ref[0, qrows, :] = (o / den).astype(BF16)


def _na_groups(rows):
    wr = min(NA_WIN_R, rows)
    G = 4 if rows % 4 == 0 else 1
    kr = min(rows, wr + G - 1)
    layouts, cls = [], []
    for g in range(rows // G):
        base = min(max(g * G - wr // 2, 0), rows - kr)
        lay = []
        for rr in range(G):
            r = g * G + rr
            r0 = min(max(r - wr // 2, 0), rows - wr)
            lay.append(tuple((base + kk - r + NA_WIN_R - 1) if r0 <= base + kk < r0 + wr else -1 for kk in range(kr)))
        lay = tuple(lay)
        if lay not in layouts:
            layouts.append(lay)
        cls.append(layouts.index(lay))
    return G, kr, wr, tuple(layouts), tuple(cls)


def _na_rel_blocks(rpb):
    H = rpb.shape[0]
    W = GRID_W
    pad = W - NA_WIN_C
    rp = rpb.astype(F32)
    wide = jnp.concatenate([jnp.repeat(rp[..., :1], pad, axis=-1), rp, jnp.repeat(rp[..., -1:], pad, axis=-1)], -1)
    qi, kcol = np.arange(W)[:, None], np.arange(W)[None, :]
    onehot = (np.arange(2 * W - 1)[:, None, None] == (kcol - qi + W - 1)[None]).astype(np.float32)
    toe = jnp.dot(wide.reshape(-1, 2 * W - 1), jnp.asarray(onehot.reshape(2 * W - 1, W * W)),
                  precision=lax.Precision.HIGHEST).reshape(H, -1, W, W)
    c0 = np.clip(qi - NA_WIN_C // 2, 0, W - NA_WIN_C)
    ok = (kcol >= c0) & (kcol < c0 + NA_WIN_C)
    return jnp.where(jnp.asarray(ok), toe, NEG)


def _na_core(y, rpb, L):
    B, T, n3 = y.shape
    D = n3 // 3
    H = NA_HEADS
    dh = D // H
    rows = (T - L) // GRID_W
    G, kr, wr, layouts, cls = _na_groups(rows)
    toe = _na_rel_blocks(rpb)
    blk = lambda seg: pl.BlockSpec((1, T, dh), lambda h, b, seg=seg: (b, 0, seg * H + h))
    return pl.pallas_call(
        functools.partial(_na_kernel, L=L, rows=rows, wr=wr, G=G, kr=kr, layouts=layouts, cls=cls, scale=dh ** -0.5),
        out_shape=jax.ShapeDtypeStruct((B, T, D), BF16),
        grid=(H, B),
        in_specs=[blk(0), blk(1), blk(2),
                  pl.BlockSpec((1,) + toe.shape[1:], lambda h, b: (h, 0, 0, 0))],
        out_specs=pl.BlockSpec((1, T, dh), lambda h, b: (b, 0, h)),
        scratch_shapes=[pltpu.VMEM((len(layouts), G * GRID_W, kr * GRID_W), F32)],
        compiler_params=_cparams(("arbitrary", "arbitrary")),
        name="na_core",
    )(y, y, y, toe)


def _outproj_kernel(u_ref, wo_ref, h_ref, g_ref, mc_ref, ml_ref, wr_ref, h1_ref, f_ref, lg_ref, *, L, tm, nsub):
    E = lg_ref.shape[2]
    ts = tm // nsub
    for sub in range(nsub):
        rs = slice(sub * ts, (sub + 1) * ts)
        row0 = pl.program_id(1) * tm + sub * ts
        g1 = _mod_rows(mc_ref, ml_ref, row0, ts, L, 2)
        sh2 = _mod_rows(mc_ref, ml_ref, row0, ts, L, 3)
        sc2 = _mod_rows(mc_ref, ml_ref, row0, ts, L, 4)
        h1 = h_ref[0, rs, :] + g1 * _dot(u_ref[0, rs, :], wo_ref[...])
        h1_ref[0, rs, :] = h1
        f = _norm_mod(h1, g_ref[...], sh2, sc2)
        f_hi = f.astype(BF16)
        f_ref[0, rs, :] = f_hi
        r1 = _dot(f_hi, wr_ref[...])
        r2 = _dot((f - f_hi.astype(F32)).astype(BF16), wr_ref[...])
        lg_ref[0, rs, :] = r1[:, :E] + r1[:, E:] + r2[:, :E]


def _outproj(u, wo, h, g, mc, ml, w_router, L):
    B, T, D = h.shape
    E = w_router.shape[1]
    wr_hi = w_router.astype(BF16)
    wr_lo = (w_router - wr_hi.astype(F32)).astype(BF16)
    wr2 = jnp.concatenate([wr_hi, wr_lo], axis=1)
    tm = _pick(T, (384, 256, 128, 64))
    nsub = 2 if tm % 32 == 0 else 1
    return pl.pallas_call(
        functools.partial(_outproj_kernel, L=L, tm=tm, nsub=nsub),
        out_shape=(jax.ShapeDtypeStruct((B, T, D), F32), jax.ShapeDtypeStruct((B, T, D), BF16),
                   jax.ShapeDtypeStruct((B, T, E), F32)),
        grid=(B, T // tm),
        in_specs=[pl.BlockSpec((1, tm, D), lambda b, i: (b, i, 0)),
                  pl.BlockSpec((D, D), lambda b, i: (0, 0)),
                  pl.BlockSpec((1, tm, D), lambda b, i: (b, i, 0)),
                  pl.BlockSpec((1, D), lambda b, i: (0, 0)),
                  pl.BlockSpec((6, D), lambda b, i: (0, 0)),
                  pl.BlockSpec((1, 6, D), lambda b, i: (b, 0, 0)),
                  pl.BlockSpec((D, 2 * E), lambda b, i: (0, 0))],
        out_specs=(pl.BlockSpec((1, tm, D), lambda b, i: (b, i, 0)),
                   pl.BlockSpec((1, tm, D), lambda b, i: (b, i, 0)),
                   pl.BlockSpec((1, tm, E), lambda b, i: (b, i, 0))),
        compiler_params=_cparams(("parallel", "parallel")),
        name="outproj",
    )(u, wo, h, g.reshape(1, D), mc, ml, wr2)


def _route_kernel(lg_ref, slot_ref, aff_ref, before_scr, *, L, caps):
    x = lg_ref[...]
    B, E, T = x.shape
    e = jnp.exp(x - jnp.max(x, axis=1, keepdims=True))
    aff = e / jnp.sum(e, axis=1, keepdims=True)
    aff_ref[...] = aff
    nmax = before_scr.shape[0]
    rb = _pick(nmax, (256, 128, 64))

    def fill(i, carry):
        r = i * rb + lax.broadcasted_iota(jnp.int32, (rb, nmax), 0)
        before_scr[pl.ds(pl.multiple_of(i * rb, rb), rb), :] = jnp.where(
            r < lax.broadcasted_iota(jnp.int32, (rb, nmax), 1), 1.0, 0.0).astype(BF16)
        return carry

    lax.fori_loop(0, nmax // rb, fill, 0)
    base = 0
    for lo, hi, cap in ((L, T, caps[1]), (0, L, caps[0])):
        n = hi - lo
        before = before_scr[0:n, 0:n]
        a = aff[:, :, lo:hi].reshape(B * E, n)
        bits = lax.bitcast_convert_type(a, jnp.int32)

        def search(it, thr, bits=bits, cap=cap):
            cand = thr | lax.shift_left(jnp.int32(1), 30 - it)
            cnt = jnp.sum(jnp.where(bits >= cand, 1.0, 0.0), axis=1, keepdims=True)
            return jnp.where(cnt >= cap, cand, thr)

        thr = lax.fori_loop(0, 31, search, jnp.zeros((B * E, 1), jnp.int32))
        gt = bits > thr
        eq = bits == thr
        need = cap - jnp.sum(jnp.where(gt, 1.0, 0.0), axis=1, keepdims=True)
        rank_eq = _dot(jnp.where(eq, 1.0, 0.0).astype(BF16), before)
        sel = gt | (eq & (rank_eq < need))
        pos = _dot(jnp.where(sel, 1.0, 0.0).astype(BF16), before)
        slot = jnp.where(sel, pos.astype(jnp.int32) + base, -1)
        slot_ref[:, :, lo:hi] = slot.reshape(B, E, n)
        base += cap


def _route(logits_t, L, caps):
    B, E, T = logits_t.shape
    return pl.pallas_call(
        functools.partial(_route_kernel, L=L, caps=caps),
        out_shape=(jax.ShapeDtypeStruct((B, E, T), jnp.int32), jax.ShapeDtypeStruct((B, E, T), F32)),
        scratch_shapes=[pltpu.VMEM((max(L, T - L), max(L, T - L)), BF16)],
        compiler_params=pltpu.CompilerParams(vmem_limit_bytes=VMEM_LIMIT_BYTES_V7X),
        name="route",
    )(logits_t)


def _gather_kernel(slot_ref, aff_ref, f_ref, x_ref, gate_ref, *, L, caps):
    T = f_ref.shape[1]
    capc, capl = caps
    for lo, hi, r0, r1 in ((L, T, 0, capl), (0, L, capl, capl + capc)):
        hit = (lax.broadcasted_iota(jnp.int32, (r1 - r0, hi - lo), 0) + r0) == slot_ref[0, 0, :, lo:hi]
        x_ref[0, 0, r0:r1, :] = _dot(jnp.where(hit, 1.0, 0.0).astype(BF16), f_ref[0, lo:hi, :]).astype(BF16)
        gate = jnp.sum(jnp.where(hit, aff_ref[0, 0, :, lo:hi], 0.0), axis=1, keepdims=True)
        gate_ref[0, 0, r0:r1, :] = jnp.broadcast_to(gate, (r1 - r0, gate_ref.shape[3]))


def _gather(slot, aff, f, L, caps):
    B, E, T = slot.shape
    D = f.shape[2]
    R = caps[0] + caps[1]
    return pl.pallas_call(
        functools.partial(_gather_kernel, L=L, caps=caps),
        out_shape=(jax.ShapeDtypeStruct((B, E, R, D), BF16), jax.ShapeDtypeStruct((B, E, R, LANES_V7X), F32)),
        grid=(B, E),
        in_specs=[pl.BlockSpec((1, 1, 1, T), lambda b, e: (b, e, 0, 0)),
                  pl.BlockSpec((1, 1, 1, T), lambda b, e: (b, e, 0, 0)),
                  pl.BlockSpec((1, T, D), lambda b, e: (b, 0, 0))],
        out_specs=(pl.BlockSpec((1, 1, R, D), lambda b, e: (b, e, 0, 0)),
                   pl.BlockSpec((1, 1, R, LANES_V7X), lambda b, e: (b, e, 0, 0))),
        compiler_params=_cparams(("parallel", "arbitrary")),
        name="moe_gather",
    )(slot.reshape(B, E, 1, T), aff.reshape(B, E, 1, T), f)


def _ffn_kernel(x_ref, gate_ref, w1_ref, w3_ref, w2_ref, y_ref):
    bs, _, R, D = x_ref.shape
    x = x_ref[...].reshape(bs * R, D)
    u = _dot(x, w1_ref[0])
    g = _dot(x, w3_ref[0])
    hm = (u * jax.nn.sigmoid(u) * g).astype(BF16)
    gate = gate_ref[...].reshape(bs * R, gate_ref.shape[3])[:, 0:1]
    y_ref[...] = (_dot(hm, w2_ref[0]) * gate).astype(BF16).reshape(bs, 1, R, D)


def _ffn(xg, gate, w1, w3, w2):
    B, E, R, D = xg.shape
    FF = w1.shape[2]
    bs = _pick(B, (2, 1))
    return pl.pallas_call(
        _ffn_kernel,
        out_shape=jax.ShapeDtypeStruct((B, E, R, D), BF16),
        grid=(E, B // bs),
        in_specs=[pl.BlockSpec((bs, 1, R, D), lambda e, b: (b, e, 0, 0)),
                  pl.BlockSpec((bs, 1, R, LANES_V7X), lambda e, b: (b, e, 0, 0)),
                  pl.BlockSpec((1, D, FF), lambda e, b: (e, 0, 0)),
                  pl.BlockSpec((1, D, FF), lambda e, b: (e, 0, 0)),
                  pl.BlockSpec((1, FF, D), lambda e, b: (e, 0, 0))],
        out_specs=pl.BlockSpec((bs, 1, R, D), lambda e, b: (b, e, 0, 0)),
        compiler_params=_cparams(("parallel", "arbitrary")),
        name="moe_ffn",
    )(xg, gate, w1, w3, w2)


def _combine_kernel(slotc_ref, y_ref, h1_ref, mc_ref, ml_ref, o_ref, *, L, tt, caps):
    E = y_ref.shape[1]
    capc, capl = caps
    i = pl.program_id(2)
    row0 = i * tt
    sc = slotc_ref[0]
    g2 = _mod_rows(mc_ref, ml_ref, row0, tt, L, 5)

    @pl.when(row0 >= L)
    def _():
        jl = lax.broadcasted_iota(jnp.int32, (1, capl), 1).astype(F32)
        hit = jnp.concatenate([jnp.where(sc[:, e:e + 1] == jl, 1.0, 0.0).astype(BF16) for e in range(E)], axis=1)
        yl = y_ref[0, :, 0:capl, :].reshape(E * capl, y_ref.shape[3])
        o_ref[0] = h1_ref[0] + g2 * _dot(hit, yl)

    @pl.when(row0 < L)
    def _():
        jc = lax.broadcasted_iota(jnp.int32, (1, E * capc), 1).astype(F32)
        hit = jnp.zeros((tt, E * capc), F32)
        for e in range(E):
            se = sc[:, e:e + 1]
            hit = hit + jnp.where(jnp.where(se >= 0, se + (e * capc - capl), -1.0) == jc, 1.0, 0.0)
        yc = y_ref[0, :, capl:capl + capc, :].reshape(E * capc, y_ref.shape[3])
        o_ref[0] = h1_ref[0] + g2 * _dot(hit.astype(BF16), yc)


def _combine(slot_c, yff, h1, mc, ml, L, caps):
    B, T, D = h1.shape
    E, R = yff.shape[1], yff.shape[2]
    tt = _pick(L, (256, 128, 64))
    td = _pick(D, (1024, 512, 256, 128))
    return pl.pallas_call(
        functools.partial(_combine_kernel, L=L, tt=tt, caps=caps),
        out_shape=jax.ShapeDtypeStruct((B, T, D), F32),
        grid=(B, D // td, T // tt),
        in_specs=[pl.BlockSpec((1, tt, E), lambda b, j, i: (b, i, 0)),
                  pl.BlockSpec((1, E, R, td), lambda b, j, i: (b, 0, 0, j)),
                  pl.BlockSpec((1, tt, td), lambda b, j, i: (b, i, j)),
                  pl.BlockSpec((6, td), lambda b, j, i: (0, j)),
                  pl.BlockSpec((1, 6, td), lambda b, j, i: (b, 0, j))],
        out_specs=pl.BlockSpec((1, tt, td), lambda b, j, i: (b, i, j)),
        compiler_params=_cparams(("parallel", "parallel", "arbitrary")),
        name="moe_combine",
    )(slot_c, yff, h1, mc, ml)


def _moe(f, logits, h1, mc, ml, w1, w3, w2, L):
    B, T, D = h1.shape
    caps = tuple(max(1, EC_CAPACITY * n // N_EXPERTS) for n in (L, T - L))
    slot, aff = _route(jnp.swapaxes(logits, 1, 2), L, caps)
    xg, gate = _gather(slot, aff, f, L, caps)
    yff = _ffn(xg, gate, w1, w3, w2)
    slot_c = jnp.swapaxes(slot, 1, 2).astype(F32)
    return _combine(slot_c, yff, h1, mc, ml, L, caps)


def _final_kernel(h_ref, g_ref, o_ref):
    h = h_ref[0]
    o_ref[0] = h * lax.rsqrt(jnp.mean(h * h, axis=-1, keepdims=True) + EPS) * g_ref[...]


def _final_norm(h, g, L):
    B, T, D = h.shape
    tr = _pick(L, (256, 128, 64))
    off = L // tr
    return pl.pallas_call(
        _final_kernel,
        out_shape=jax.ShapeDtypeStruct((B, T - L, D), F32),
        grid=(B, (T - L) // tr),
        in_specs=[pl.BlockSpec((1, tr, D), lambda b, i: (b, i + off, 0)),
                  pl.BlockSpec((1, D), lambda b, i: (0, 0))],
        out_specs=pl.BlockSpec((1, tr, D), lambda b, i: (b, i, 0)),
        compiler_params=_cparams(("parallel", "parallel")),
        name="final_norm",
    )(h, g.reshape(1, D))


def kernel(x, c, ctx, c_ctx, ada_w, ada_b, norm_g, final_g, ret_wq, ret_wk, ret_wv, ret_wg, ret_wo, ret_decay, ret_gn_w, ret_gn_b, ml_wq, ml_wk, ml_wv, ml_wog, ml_wgate, ml_bgate, ml_norm_w, ml_wout, na_wqkv, na_rpb, na_wo, hg_wq, hg_wi, hg_wf, hg_wg, hg_norm_w, hg_wo, hg_lb, moe_router, moe_w1, moe_w3, moe_w2):
    B, N, D = x.shape
    L = ctx.shape[1]
    depth = ada_w.shape[0]
    h = jnp.concatenate([ctx, x], axis=1)
    mod_lat, mod_ctx = _mod_vectors(c, c_ctx, ada_w, ada_b)
    lb_all = jnp.cumsum(jax.nn.softmax(hg_lb.astype(F32), axis=0), axis=0)
    zero_b = lambda n: jnp.zeros((1, n), F32)
    gpad = LANES_V7X - 4 * ML_HEADS
    for i in range(depth):
        mc, ml = mod_ctx[i], mod_lat[i]
        kind = i % N_MIXERS
        if kind == 0:
            w = jnp.concatenate([ret_wq, ret_wk, ret_wv, ret_wg], 1).astype(BF16)
            y = _proj(h, norm_g[i, 0], mc, ml, w, zero_b(w.shape[1]), L)
            u = _retention_core(y, ret_decay, ret_gn_w, ret_gn_b, L)
            wo = ret_wo
        elif kind == 1:
            w = jnp.concatenate([ml_wq, ml_wk, ml_wv, ml_wog, ml_wgate[0], ml_wgate[1], jnp.zeros((D, gpad), F32)],
                                1).astype(BF16)
            bias = jnp.concatenate([jnp.zeros((3 * D,), F32), ml_bgate[0], ml_bgate[1], jnp.zeros((gpad,), F32)])[None]
            y = _proj(h, norm_g[i, 0], mc, ml, w, bias, L)
            u = _mlstm_core(y, ml_norm_w, D, L)
            wo = ml_wout
        elif kind == 2:
            w = na_wqkv.astype(BF16)
            y = _proj(h, norm_g[i, 0], mc, ml, w, zero_b(w.shape[1]), L, out_dtype=BF16)
            u = _na_core(y, na_rpb, L)
            wo = na_wo
        else:
            w = jnp.concatenate([hg_wq, hg_wi, hg_wf[0], hg_wf[1], hg_wg], 1).astype(BF16)
            y = _proj(h, norm_g[i, 0], mc, ml, w, zero_b(w.shape[1]), L)
            u = _hgrn2_core(y, lb_all[i] - lb_all[0], hg_norm_w, L)
            wo = hg_wo
        h1, f, logits = _outproj(u, wo.astype(BF16), h, norm_g[i, 1], mc, ml, moe_router[i], L)
        h = _moe(f, logits, h1, mc, ml, moe_w1[i].astype(BF16), moe_w3[i].astype(BF16), moe_w2[i].astype(BF16), L)
    return _final_norm(h, final_g, L)
```

```python
import functools

import numpy as np
import jax
import jax.numpy as jnp
from jax import lax
from jax.experimental import pallas as pl
from jax.experimental.pallas import tpu as pltpu

GRID_W = 64
EPS = 1e-6
NEG = -1e30
CHUNK = 64
ROPE_THETA = 10000.0
RET_HEADS = 8
ML_HEADS = 8
NA_HEADS = 16
NA_WIN_R = 8
NA_WIN_C = 16
HG_HEADS = 16
N_EXPERTS = 16
EC_CAPACITY = 2
N_MIXERS = 4

VMEM_LIMIT_BYTES_V7X = 58 * 1024 * 1024
LANES_V7X = 128

F32 = jnp.float32
BF16 = jnp.bfloat16


def _cparams(sem):
    return pltpu.CompilerParams(dimension_semantics=sem, vmem_limit_bytes=VMEM_LIMIT_BYTES_V7X)


def _pick(n, prefs):
    for p in prefs:
        if n % p == 0:
            return p
    return n


def _dot(a, b):
    return jnp.dot(a, b, preferred_element_type=F32)


def _dot_nt(a, b):
    return lax.dot_general(a, b, (((1,), (1,)), ((), ())), preferred_element_type=F32)


def _dot_tn(a, b):
    return lax.dot_general(a, b, (((0,), (0,)), ((), ())), preferred_element_type=F32)


def _norm_mod(h, g, sh, sc):
    ms = jnp.mean(h * h, axis=-1, keepdims=True)
    y = h * lax.rsqrt(ms + EPS) * g
    return y * (1.0 + sc) + sh


def _mod_rows(mc_ref, ml_ref, row0, tm, L, k):
    row = row0 + lax.broadcasted_iota(jnp.int32, (tm, 1), 0)
    return jnp.where(row < L, mc_ref[k:k + 1, :], ml_ref[0, k:k + 1, :])


def _seq_block(T, L):
    cb = _pick(L, (256, 128, 64))
    assert L % cb == 0 and (T - L) % cb == 0
    return cb


def _bwd_block(s, nctx, ntot):
    return jnp.where(s < nctx, nctx - 1 - s, ntot - 1 - (s - nctx))


def _mod_kernel(c_ref, w_ref, b_ref, o_ref):
    c = c_ref[...]
    s = (c * jax.nn.sigmoid(c)).astype(BF16)
    o_ref[0] = _dot(s, w_ref[0].astype(BF16)) + b_ref[0]


def _mod_vectors(c, c_ctx, ada_w, ada_b):
    B, D = c.shape
    depth, _, n6 = ada_w.shape
    mp = -(-(B + 1) // 8) * 8
    c_all = jnp.concatenate([c, c_ctx[None], jnp.zeros((mp - B - 1, D), F32)], 0)
    tn = _pick(n6, (1024, 512, 256, 128))
    out = pl.pallas_call(
        _mod_kernel,
        out_shape=jax.ShapeDtypeStruct((depth, mp, n6), F32),
        grid=(depth, n6 // tn),
        in_specs=[pl.BlockSpec((mp, D), lambda i, j: (0, 0)),
                  pl.BlockSpec((1, D, tn), lambda i, j: (i, 0, j)),
                  pl.BlockSpec((1, 1, tn), lambda i, j: (i, 0, j))],
        out_specs=pl.BlockSpec((1, mp, tn), lambda i, j: (i, 0, j)),
        compiler_params=_cparams(("parallel", "parallel")),
        name="mod_vectors",
    )(c_all, ada_w, ada_b.reshape(depth, 1, n6))
    mod_lat = out[:, :B].reshape(depth, B, 6, D)
    mod_ctx = out[:, B].reshape(depth, 6, D)
    return mod_lat, mod_ctx


def _proj_kernel(h_ref, g_ref, mc_ref, ml_ref, w_ref, b_ref, o_ref, a_scr, *, L, tm, ts):
    i = pl.program_id(1)
    j = pl.program_id(2)

    @pl.when(j == 0)
    def _():
        for sub in range(tm // ts):
            rs = slice(sub * ts, (sub + 1) * ts)
            sh = _mod_rows(mc_ref, ml_ref, i * tm + sub * ts, ts, L, 0)
            sc = _mod_rows(mc_ref, ml_ref, i * tm + sub * ts, ts, L, 1)
            a = _norm_mod(h_ref[0, rs, :], g_ref[...], sh, sc).astype(BF16)
            a_scr[rs, :] = a
            o_ref[0, rs, :] = (_dot(a, w_ref[...]) + b_ref[...]).astype(o_ref.dtype)

    @pl.when(j > 0)
    def _():
        o_ref[0] = (_dot(a_scr[...], w_ref[...]) + b_ref[...]).astype(o_ref.dtype)


def _proj(h, g, mc, ml, w, bias, L, out_dtype=F32):
    B, T, D = h.shape
    n = w.shape[1]
    tm = _pick(T, (1152, 768, 576, 384, 256, 128, 64))
    ts = _pick(tm, (128, 64))
    tn = _pick(n, (1024, 896, 512, 256, 128))
    return pl.pallas_call(
        functools.partial(_proj_kernel, L=L, tm=tm, ts=ts),
        out_shape=jax.ShapeDtypeStruct((B, T, n), out_dtype),
        grid=(B, T // tm, n // tn),
        in_specs=[pl.BlockSpec((1, tm, D), lambda b, i, j: (b, i, 0)),
                  pl.BlockSpec((1, D), lambda b, i, j: (0, 0)),
                  pl.BlockSpec((6, D), lambda b, i, j: (0, 0)),
                  pl.BlockSpec((1, 6, D), lambda b, i, j: (b, 0, 0)),
                  pl.BlockSpec((D, tn), lambda b, i, j: (0, j)),
                  pl.BlockSpec((1, tn), lambda b, i, j: (0, j))],
        out_specs=pl.BlockSpec((1, tm, tn), lambda b, i, j: (b, i, j)),
        scratch_shapes=[pltpu.VMEM((tm, D), BF16)],
        compiler_params=_cparams(("parallel", "parallel", "arbitrary")),
        name="proj",
    )(h, g.reshape(1, D), mc, ml, w, bias)


def _head_norm(o, w, b, center):
    if center:
        o = o - jnp.mean(o, axis=-1, keepdims=True)
    o = o * lax.rsqrt(jnp.mean(o * o, axis=-1, keepdims=True) + EPS)
    o = o * w
    if b is not None:
        o = o + b
    return o


def _ret_kernel(dec_ref, q_ref, k_ref, v_ref, g_ref, cos_ref, sin_ref, gw_ref, gb_ref, u_ref,
                s_scr, dm_scr, of_scr, ob_scr, *, dk, C, nctx, ntot, scale):
    h = pl.program_id(1)
    half = dk // 2
    ri = lax.broadcasted_iota(jnp.int32, (C, C), 0)
    ci = lax.broadcasted_iota(jnp.int32, (C, C), 1)
    dif = (ri - ci).astype(F32)
    pos = lax.broadcasted_iota(jnp.int32, (C, 1), 0).astype(F32)
    lg_f = jax.nn.log_sigmoid(jnp.full((1, 1), dec_ref[0, h], F32))
    lg_b = jax.nn.log_sigmoid(jnp.full((1, 1), dec_ref[1, h], F32))
    dm_scr[0] = jnp.where(ri >= ci, jnp.exp(lg_f * dif), 0.0)
    dm_scr[1] = jnp.where(ri <= ci, jnp.exp(-lg_b * dif), 0.0)
    q_dec = (jnp.exp(lg_f * (pos + 1.0)), jnp.exp(lg_b * (C - pos)))
    k_dec = (jnp.exp(lg_f * (C - 1.0 - pos)), jnp.exp(lg_b * pos))
    s_dec = (jnp.exp(lg_f * C), jnp.exp(lg_b * C))
    s_scr[...] = jnp.zeros(s_scr.shape, F32)

    def step(rows, d):
        cos = cos_ref[rows, :]
        sin = sin_ref[rows, :]
        q = q_ref[0, rows, :]
        k = k_ref[0, rows, :]
        q1, q2 = q[:, :half], q[:, half:]
        k1, k2 = k[:, :half], k[:, half:]
        q = jnp.concatenate([q1 * cos - q2 * sin, q1 * sin + q2 * cos], axis=1)
        k = jnp.concatenate([k1 * cos - k2 * sin, k1 * sin + k2 * cos], axis=1) * scale
        vb = v_ref[0, rows, :].astype(BF16)
        a = _dot_nt(q.astype(BF16), k.astype(BF16)) * dm_scr[d]
        st = s_scr[d]
        o = _dot(a.astype(BF16), vb) + _dot_nt((q * q_dec[d]).astype(BF16), st.astype(BF16))
        s_scr[d] = s_dec[d] * st + _dot_tn(vb, (k * k_dec[d]).astype(BF16))
        return o

    def body(s, carry):
        rf = pl.ds(pl.multiple_of(s * C, C), C)
        of_scr[rf, :] = step(rf, 0)
        rb = pl.ds(pl.multiple_of(_bwd_block(s, nctx, ntot) * C, C), C)
        ob_scr[rb, :] = step(rb, 1)
        return carry

    lax.fori_loop(0, ntot, body, 0, unroll=True)

    def fin(i, carry):
        rows = pl.ds(pl.multiple_of(i * C, C), C)
        o = _head_norm(of_scr[rows, :] + ob_scr[rows, :], gw_ref[...], gb_ref[...], True)
        g = g_ref[0, rows, :]
        u_ref[0, rows, :] = (g * jax.nn.sigmoid(g) * o).astype(BF16)
        return carry

    lax.fori_loop(0, ntot, fin, 0)


def _rope_tables(N, L, dk):
    t = np.arange(N)
    row = (t // GRID_W).astype(np.float32)
    col = (t % GRID_W).astype(np.float32)
    quarter = dk // 4
    inv = (ROPE_THETA ** (-np.arange(quarter, dtype=np.float32) / quarter)).astype(np.float32)
    ang = np.concatenate([row[:, None] * inv, col[:, None] * inv], -1).astype(np.float32)
    cos = np.concatenate([np.ones((L, dk // 2), np.float32), np.cos(ang)], 0)
    sin = np.concatenate([np.zeros((L, dk // 2), np.float32), np.sin(ang)], 0)
    return jnp.asarray(cos, F32), jnp.asarray(sin, F32)


def _retention_core(y, decay, gn_w, gn_b, L):
    B, T, n4 = y.shape
    D = n4 // 4
    H = RET_HEADS
    dk = D // H
    dv = dk
    C = _seq_block(T, L)
    cos, sin = _rope_tables(T - L, L, dk)
    blk = lambda seg: pl.BlockSpec((1, T, dk), lambda b, h, seg=seg: (b, 0, seg * H + h))
    return pl.pallas_call(
        functools.partial(_ret_kernel, dk=dk, C=C, nctx=L // C, ntot=T // C, scale=dk ** -0.5),
        out_shape=jax.ShapeDtypeStruct((B, T, D), BF16),
        grid=(B, H),
        in_specs=[pl.BlockSpec(memory_space=pltpu.SMEM),
                  blk(0), blk(1), blk(2), blk(3),
                  pl.BlockSpec((T, dk // 2), lambda b, h: (0, 0)),
                  pl.BlockSpec((T, dk // 2), lambda b, h: (0, 0)),
                  pl.BlockSpec((1, dv), lambda b, h: (0, h)),
                  pl.BlockSpec((1, dv), lambda b, h: (0, h))],
        out_specs=pl.BlockSpec((1, T, dv), lambda b, h: (b, 0, h)),
        scratch_shapes=[pltpu.VMEM((2, dv, dk), F32), pltpu.VMEM((2, C, C), F32),
                        pltpu.VMEM((T, dv), F32), pltpu.VMEM((T, dv), F32)],
        compiler_params=_cparams(("parallel", "parallel")),
        name="retention_core",
    )(decay.astype(F32), y, y, y, y, cos, sin, gn_w.reshape(1, D), gn_b.reshape(1, D))


def _chunk_cumsum(g, C, reverse):
    n = g.shape[0]
    pos = lax.broadcasted_iota(jnp.int32, (n, 1), 0) & (C - 1)
    x = g
    s = 1
    while s < C:
        if reverse:
            x = x + jnp.where(pos < C - s, pltpu.roll(x, n - s, axis=0), 0.0)
        else:
            x = x + jnp.where(pos >= s, pltpu.roll(x, s, axis=0), 0.0)
        s *= 2
    return x


def _hg_kernel(q_ref, i_ref, zf_ref, zb_ref, g_ref, lb_ref, nw_ref, u_ref,
               s_scr, of_scr, ob_scr, *, hp, dk, dv, C, CB, nctx, ntot, scale):
    nb = CB // C
    sh = C.bit_length() - 1
    ri = lax.broadcasted_iota(jnp.int32, (CB, CB), 0)
    ci = lax.broadcasted_iota(jnp.int32, (CB, CB), 1)
    same = (ri >> sh) == (ci >> sh)
    m_lo = same & (ri >= ci)
    m_up = same & (ri <= ci)
    s_scr[...] = jnp.zeros(s_scr.shape, F32)

    def block(rows, hh, z_ref, mask, reverse, o_scr, idx):
        cs = slice(hh * dk, (hh + 1) * dk)
        vs = slice(hh * dv, (hh + 1) * dv)
        lb = lb_ref[:, cs]
        qr = q_ref[0, rows, cs]
        q = qr * jax.nn.sigmoid(qr) * scale
        z = z_ref[0, rows, cs]
        sig = jax.nn.sigmoid(z)
        f = lb + (1.0 - lb) * sig
        k = (1.0 - lb) * (1.0 - sig)
        vb = i_ref[0, rows, vs].astype(BF16)
        b = _chunk_cumsum(jnp.log(f), C, reverse)
        b3 = b.reshape(nb, C, dk)
        e = 0 if reverse else C - 1
        b_end3 = b3[:, e:e + 1, :]
        q_in = (q * jnp.exp(b)).astype(BF16)
        k_in = (k * jnp.exp(-b)).astype(BF16)
        k_end = (k.reshape(nb, C, dk) * jnp.exp(b_end3 - b3)).reshape(CB, dk).astype(BF16)
        a = jnp.where(mask, _dot_nt(q_in, k_in), 0.0)
        o = _dot(a.astype(BF16), vb)
        st = s_scr[idx]
        inter = [None] * nb
        for c in (range(nb - 1, -1, -1) if reverse else range(nb)):
            rc = slice(c * C, (c + 1) * C)
            inter[c] = _dot_nt(q_in[rc], st.astype(BF16))
            st = jnp.exp(b_end3[c]) * st + _dot_tn(vb[rc], k_end[rc])
        s_scr[idx] = st
        o_scr[rows, vs] = o + jnp.concatenate(inter, axis=0)

    def body(s, carry):
        rf = pl.ds(pl.multiple_of(s * CB, CB), CB)
        rb = pl.ds(pl.multiple_of(_bwd_block(s, nctx, ntot) * CB, CB), CB)
        for hh in range(hp):
            block(rf, hh, zf_ref, m_lo, False, of_scr, 2 * hh)
            block(rb, hh, zb_ref, m_up, True, ob_scr, 2 * hh + 1)
        return carry

    lax.fori_loop(0, ntot, body, 0)

    def fin(i, carry):
        rows = pl.ds(pl.multiple_of(i * CB, CB), CB)
        for hh in range(hp):
            cs = slice(hh * dv, (hh + 1) * dv)
            o = _head_norm(of_scr[rows, cs] + ob_scr[rows, cs], nw_ref[:, cs], None, False)
            g = g_ref[0, rows, cs]
            u_ref[0, rows, cs] = (g * jax.nn.sigmoid(g) * o).astype(BF16)
        return carry

    lax.fori_loop(0, ntot, fin, 0)


def _hgrn2_core(y, lb, norm_w, L):
    B, T, n5 = y.shape
    D = n5 // 5
    H = HG_HEADS
    dk = D // H
    dv = dk
    hp = 2 if H % 2 == 0 else 1
    G = H // hp
    CB = _seq_block(T, L)
    C = min(CHUNK, CB)
    blk = lambda seg: pl.BlockSpec((1, T, hp * dk), lambda b, h, seg=seg: (b, 0, seg * G + h))
    return pl.pallas_call(
        functools.partial(_hg_kernel, hp=hp, dk=dk, dv=dv, C=C, CB=CB, nctx=L // CB, ntot=T // CB, scale=dk ** -0.5),
        out_shape=jax.ShapeDtypeStruct((B, T, D), BF16),
        grid=(B, G),
        in_specs=[blk(0), blk(1), blk(2), blk(3), blk(4),
                  pl.BlockSpec((1, hp * dk), lambda b, h: (0, h)),
                  pl.BlockSpec((1, hp * dv), lambda b, h: (0, h))],
        out_specs=pl.BlockSpec((1, T, hp * dv), lambda b, h: (b, 0, h)),
        scratch_shapes=[pltpu.VMEM((2 * hp, dv, dk), F32), pltpu.VMEM((T, hp * dv), F32),
                        pltpu.VMEM((T, hp * dv), F32)],
        compiler_params=_cparams(("parallel", "parallel")),
        name="hgrn2_core",
    )(y, y, y, y, y, lb.reshape(1, D).astype(F32), norm_w.reshape(1, D))


def _tri(C):
    r = lax.broadcasted_iota(jnp.int32, (C, C), 0)
    c = lax.broadcasted_iota(jnp.int32, (C, C), 1)
    return r >= c, r <= c


def _ml_kernel(q_ref, k_ref, v_ref, og_ref, gt_ref, nw_ref, u_ref, c_scr, n_scr, m_scr, of_scr, ob_scr,
               *, H, C, nctx, ntot, scale):
    h = pl.program_id(1)
    lower, upper = _tri(C)
    c_scr[...] = jnp.zeros(c_scr.shape, F32)
    n_scr[...] = jnp.zeros(n_scr.shape, F32)
    m_scr[...] = jnp.zeros(m_scr.shape, F32)
    gw = gt_ref.shape[2]
    lane = lax.broadcasted_iota(jnp.int32, (C, gw), 1)

    def chunk(start, d, mask, end_row, o_scr):
        rows = pl.ds(start, C)
        gt = gt_ref[0, rows, :]
        cum = _chunk_cumsum(jax.nn.log_sigmoid(gt), C, d == 1)
        b_col = jnp.sum(jnp.where(lane == h + H + 2 * H * d, cum, 0.0), axis=1, keepdims=True)
        i_col = jnp.sum(jnp.where(lane == h + 2 * H * d, gt, 0.0), axis=1, keepdims=True)
        z = jnp.where(lane == 0, b_col, jnp.where(lane == 1, i_col, 0.0))
        r = jnp.transpose(z)
        b_row, i_row = r[0:1, :], r[1:2, :]
        q = q_ref[0, rows, :]
        k = k_ref[0, rows, :] * scale
        v = v_ref[0, rows, :]
        m_prev = m_scr[d, 0:1, 0:1]
        logw = jnp.where(mask, b_col - b_row + i_row, -jnp.inf)
        log_carry = b_col + m_prev
        m_i = jnp.maximum(log_carry, jnp.max(logw, axis=1, keepdims=True))
        qb, vb = q.astype(BF16), v.astype(BF16)
        s = _dot_nt(qb, k.astype(BF16)) * jnp.exp(logw - m_i)
        a = jnp.exp(log_carry - m_i)
        cst = c_scr[d]
        nst = n_scr[d]
        num = _dot(s.astype(BF16), vb) + a * _dot(qb, cst.astype(BF16))
        den = jnp.sum(s, axis=1, keepdims=True) + a * jnp.sum(q * nst, axis=1, keepdims=True)
        o_scr[rows, :] = num / jnp.maximum(jnp.abs(den), jnp.exp(-m_i))
        b_end = b_col[end_row:end_row + 1, :]
        logw_end = b_end - b_col + i_col
        m_new = jnp.maximum(b_end + m_prev, jnp.max(logw_end, axis=0, keepdims=True))
        decay = jnp.exp(b_end + m_prev - m_new)
        kw = k * jnp.exp(logw_end - m_new)
        c_scr[d] = decay * cst + _dot_tn(kw.astype(BF16), vb)
        n_scr[d] = decay * nst + jnp.sum(kw, axis=0, keepdims=True)
        m_scr[d] = jnp.broadcast_to(m_new, m_scr.shape[1:])

    def body(s, carry):
        chunk(pl.multiple_of(s * C, C), 0, lower, C - 1, of_scr)
        chunk(pl.multiple_of(_bwd_block(s, nctx, ntot) * C, C), 1, upper, 0, ob_scr)
        return carry

    lax.fori_loop(0, ntot, body, 0, unroll=True)

    def fin(i, carry):
        rows = pl.ds(pl.multiple_of(i * C, C), C)
        o = _head_norm(of_scr[rows, :] + ob_scr[rows, :], nw_ref[...], None, True)
        u_ref[0, rows, :] = (jax.nn.sigmoid(og_ref[0, rows, :]) * o).astype(BF16)
        return carry

    lax.fori_loop(0, ntot, fin, 0)


def _mlstm_core(y, norm_w, D, L):
    B, T, n = y.shape
    H = ML_HEADS
    dv = D // H
    dk = dv // 2
    gw = n - 3 * D
    C = _seq_block(T, L)
    return pl.pallas_call(
        functools.partial(_ml_kernel, H=H, C=C, nctx=L // C, ntot=T // C, scale=dk ** -0.5),
        out_shape=jax.ShapeDtypeStruct((B, T, D), BF16),
        grid=(B, H),
        in_specs=[pl.BlockSpec((1, T, dk), lambda b, h: (b, 0, h)),
                  pl.BlockSpec((1, T, dk), lambda b, h: (b, 0, H + h)),
                  pl.BlockSpec((1, T, dv), lambda b, h: (b, 0, H + h)),
                  pl.BlockSpec((1, T, dv), lambda b, h: (b, 0, 2 * H + h)),
                  pl.BlockSpec((1, T, gw), lambda b, h: (b, 0, 3 * D // gw)),
                  pl.BlockSpec((1, dv), lambda b, h: (0, h))],
        out_specs=pl.BlockSpec((1, T, dv), lambda b, h: (b, 0, h)),
        scratch_shapes=[pltpu.VMEM((2, dk, dv), F32), pltpu.VMEM((2, 1, dk), F32), pltpu.VMEM((2, 8, LANES_V7X), F32),
                        pltpu.VMEM((T, dv), F32), pltpu.VMEM((T, dv), F32)],
        compiler_params=_cparams(("parallel", "parallel")),
        name="mlstm_core",
    )(y, y, y, y, y, norm_w.reshape(1, D))


def _na_kernel(q_ref, k_ref, v_ref, toe_ref, o_ref, bias_scr, *, L, rows, wr, G, kr, layouts, cls, scale):
    W = GRID_W

    @pl.when(pl.program_id(1) == 0)
    def _():
        for l, lay in enumerate(layouts):
            for rr, row in enumerate(lay):
                for kk, d in enumerate(row):
                    blk = toe_ref[0, d] if d >= 0 else jnp.full((W, W), NEG, F32)
                    bias_scr[l, rr * W:(rr + 1) * W, kk * W:(kk + 1) * W] = blk

    kc = k_ref[0, 0:L, :].astype(BF16)
    vc = v_ref[0, 0:L, :].astype(BF16)
    qc = q_ref[0, 0:L, :].astype(BF16)
    s = _dot_nt(qc, kc) * scale
    p = jnp.exp(s - jnp.max(s, axis=1, keepdims=True))
    o_ref[0, 0:L, :] = (_dot(p.astype(BF16), vc) / jnp.sum(p, axis=1, keepdims=True)).astype(BF16)

    for g in range(rows // G):
        base = min(max(g * G - wr // 2, 0), rows - kr)
        qrows = slice(L + g * G * W, L + (g + 1) * G * W)
        krows = slice(L + base * W, L + (base + kr) * W)
        q = q_ref[0, qrows, :].astype(BF16)
        kw = k_ref[0, krows, :].astype(BF16)
        vw = v_ref[0, krows, :].astype(BF16)
        s_win = _dot_nt(q, kw) * scale + bias_scr[cls[g]]
        s_ctx = _dot_nt(q, kc) * scale
        m = jnp.maximum(jnp.max(s_win, axis=1, keepdims=True), jnp.max(s_ctx, axis=1, keepdims=True))
        p_win = jnp.exp(s_win - m)
        p_ctx = jnp.exp(s_ctx - m)
        den = jnp.sum(p_win, axis=1, keepdims=True) + jnp.sum(p_ctx, axis=1, keepdims=True)
        o = _dot(p_win.astype(BF16), vw) + _dot(p_ctx.astype(BF16), vc)
        o_ref[0, qrows, :] = (o / den).astype(BF16)


def _na_groups(rows):
    wr = min(NA_WIN_R, rows)
    G = 4 if rows % 4 == 0 else 1
    kr = min(rows, wr + G - 1)
    layouts, cls = [], []
    for g in range(rows // G):
        base = min(max(g * G - wr // 2, 0), rows - kr)
        lay = []
        for rr in range(G):
            r = g * G + rr
            r0 = min(max(r - wr // 2, 0), rows - wr)
            lay.append(tuple((base + kk - r + NA_WIN_R - 1) if r0 <= base + kk < r0 + wr else -1 for kk in range(kr)))
        lay = tuple(lay)
        if lay not in layouts:
            layouts.append(lay)
        cls.append(layouts.index(lay))
    return G, kr, wr, tuple(layouts), tuple(cls)


def _na_rel_blocks(rpb):
    H = rpb.shape[0]
    W = GRID_W
    pad = W - NA_WIN_C
    rp = rpb.astype(F32)
    wide = jnp.concatenate([jnp.repeat(rp[..., :1], pad, axis=-1), rp, jnp.repeat(rp[..., -1:], pad, axis=-1)], -1)
    qi, kcol = np.arange(W)[:, None], np.arange(W)[None, :]
    onehot = (np.arange(2 * W - 1)[:, None, None] == (kcol - qi + W - 1)[None]).astype(np.float32)
    toe = jnp.dot(wide.reshape(-1, 2 * W - 1), jnp.asarray(onehot.reshape(2 * W - 1, W * W)),
                  precision=lax.Precision.HIGHEST).reshape(H, -1, W, W)
    c0 = np.clip(qi - NA_WIN_C // 2, 0, W - NA_WIN_C)
    ok = (kcol >= c0) & (kcol < c0 + NA_WIN_C)
    return jnp.where(jnp.asarray(ok), toe, NEG)


def _na_core(y, rpb, L):
    B, T, n3 = y.shape
    D = n3 // 3
    H = NA_HEADS
    dh = D // H
    rows = (T - L) // GRID_W
    G, kr, wr, layouts, cls = _na_groups(rows)
    toe = _na_rel_blocks(rpb)
    blk = lambda seg: pl.BlockSpec((1, T, dh), lambda h, b, seg=seg: (b, 0, seg * H + h))
    return pl.pallas_call(
        functools.partial(_na_kernel, L=L, rows=rows, wr=wr, G=G, kr=kr, layouts=layouts, cls=cls, scale=dh ** -0.5),
        out_shape=jax.ShapeDtypeStruct((B, T, D), BF16),
        grid=(H, B),
        in_specs=[blk(0), blk(1), blk(2),
                  pl.BlockSpec((1,) + toe.shape[1:], lambda h, b: (h, 0, 0, 0))],
        out_specs=pl.BlockSpec((1, T, dh), lambda h, b: (b, 0, h)),
        scratch_shapes=[pltpu.VMEM((len(layouts), G * GRID_W, kr * GRID_W), F32)],
        compiler_params=_cparams(("arbitrary", "arbitrary")),
        name="na_core",
    )(y, y, y, toe)


def _outproj_kernel(u_ref, wo_ref, h_ref, g_ref, mc_ref, ml_ref, wr_ref, h1_ref, f_ref, lg_ref, *, L, tm, nsub):
    E = lg_ref.shape[2]
    ts = tm // nsub
    for sub in range(nsub):
        rs = slice(sub * ts, (sub + 1) * ts)
        row0 = pl.program_id(1) * tm + sub * ts
        g1 = _mod_rows(mc_ref, ml_ref, row0, ts, L, 2)
        sh2 = _mod_rows(mc_ref, ml_ref, row0, ts, L, 3)
        sc2 = _mod_rows(mc_ref, ml_ref, row0, ts, L, 4)
        h1 = h_ref[0, rs, :] + g1 * _dot(u_ref[0, rs, :], wo_ref[...])
        h1_ref[0, rs, :] = h1
        f = _norm_mod(h1, g_ref[...], sh2, sc2)
        f_hi = f.astype(BF16)
        f_ref[0, rs, :] = f_hi
        r1 = _dot(f_hi, wr_ref[...])
        r2 = _dot((f - f_hi.astype(F32)).astype(BF16), wr_ref[...])
        lg_ref[0, rs, :] = r1[:, :E] + r1[:, E:] + r2[:, :E]


def _outproj(u, wo, h, g, mc, ml, w_router, L):
    B, T, D = h.shape
    E = w_router.shape[1]
    wr_hi = w_router.astype(BF16)
    wr_lo = (w_router - wr_hi.astype(F32)).astype(BF16)
    wr2 = jnp.concatenate([wr_hi, wr_lo], axis=1)
    tm = _pick(T, (384, 256, 128, 64))
    nsub = 2 if tm % 32 == 0 else 1
    return pl.pallas_call(
        functools.partial(_outproj_kernel, L=L, tm=tm, nsub=nsub),
        out_shape=(jax.ShapeDtypeStruct((B, T, D), F32), jax.ShapeDtypeStruct((B, T, D), BF16),
                   jax.ShapeDtypeStruct((B, T, E), F32)),
        grid=(B, T // tm),
        in_specs=[pl.BlockSpec((1, tm, D), lambda b, i: (b, i, 0)),
                  pl.BlockSpec((D, D), lambda b, i: (0, 0)),
                  pl.BlockSpec((1, tm, D), lambda b, i: (b, i, 0)),
                  pl.BlockSpec((1, D), lambda b, i: (0, 0)),
                  pl.BlockSpec((6, D), lambda b, i: (0, 0)),
                  pl.BlockSpec((1, 6, D), lambda b, i: (b, 0, 0)),
                  pl.BlockSpec((D, 2 * E), lambda b, i: (0, 0))],
        out_specs=(pl.BlockSpec((1, tm, D), lambda b, i: (b, i, 0)),
                   pl.BlockSpec((1, tm, D), lambda b, i: (b, i, 0)),
                   pl.BlockSpec((1, tm, E), lambda b, i: (b, i, 0))),
        compiler_params=_cparams(("parallel", "parallel")),
        name="outproj",
    )(u, wo, h, g.reshape(1, D), mc, ml, wr2)


def _route_kernel(lg_ref, slot_ref, aff_ref, before_scr, *, L, caps):
    x = lg_ref[...]
    B, E, T = x.shape
    e = jnp.exp(x - jnp.max(x, axis=1, keepdims=True))
    aff = e / jnp.sum(e, axis=1, keepdims=True)
    aff_ref[...] = aff
    nmax = before_scr.shape[0]
    rb = _pick(nmax, (256, 128, 64))

    def fill(i, carry):
        r = i * rb + lax.broadcasted_iota(jnp.int32, (rb, nmax), 0)
        before_scr[pl.ds(pl.multiple_of(i * rb, rb), rb), :] = jnp.where(
            r < lax.broadcasted_iota(jnp.int32, (rb, nmax), 1), 1.0, 0.0).astype(BF16)
        return carry

    lax.fori_loop(0, nmax // rb, fill, 0)
    base = 0
    for lo, hi, cap in ((L, T, caps[1]), (0, L, caps[0])):
        n = hi - lo
        before = before_scr[0:n, 0:n]
        a = aff[:, :, lo:hi].reshape(B * E, n)
        bits = lax.bitcast_convert_type(a, jnp.int32)

        def search(it, thr, bits=bits, cap=cap):
            cand = thr | lax.shift_left(jnp.int32(1), 30 - it)
            cnt = jnp.sum(jnp.where(bits >= cand, 1.0, 0.0), axis=1, keepdims=True)
            return jnp.where(cnt >= cap, cand, thr)

        thr = lax.fori_loop(0, 31, search, jnp.zeros((B * E, 1), jnp.int32))
        gt = bits > thr
        eq = bits == thr
        need = cap - jnp.sum(jnp.where(gt, 1.0, 0.0), axis=1, keepdims=True)
        rank_eq = _dot(jnp.where(eq, 1.0, 0.0).astype(BF16), before)
        sel = gt | (eq & (rank_eq < need))
        pos = _dot(jnp.where(sel, 1.0, 0.0).astype(BF16), before)
        slot = jnp.where(sel, pos.astype(jnp.int32) + base, -1)
        slot_ref[:, :, lo:hi] = slot.reshape(B, E, n)
        base += cap


def _route(logits_t, L, caps):
    B, E, T = logits_t.shape
    return pl.pallas_call(
        functools.partial(_route_kernel, L=L, caps=caps),
        out_shape=(jax.ShapeDtypeStruct((B, E, T), jnp.int32), jax.ShapeDtypeStruct((B, E, T), F32)),
        scratch_shapes=[pltpu.VMEM((max(L, T - L), max(L, T - L)), BF16)],
        compiler_params=pltpu.CompilerParams(vmem_limit_bytes=VMEM_LIMIT_BYTES_V7X),
        name="route",
    )(logits_t)


def _gather_kernel(slot_ref, aff_ref, f_ref, x_ref, gate_ref, *, L, caps):
    T = f_ref.shape[1]
    capc, capl = caps
    for lo, hi, r0, r1 in ((L, T, 0, capl), (0, L, capl, capl + capc)):
        hit = (lax.broadcasted_iota(jnp.int32, (r1 - r0, hi - lo), 0) + r0) == slot_ref[0, 0, :, lo:hi]
        x_ref[0, 0, r0:r1, :] = _dot(jnp.where(hit, 1.0, 0.0).astype(BF16), f_ref[0, lo:hi, :]).astype(BF16)
        gate = jnp.sum(jnp.where(hit, aff_ref[0, 0, :, lo:hi], 0.0), axis=1, keepdims=True)
        gate_ref[0, 0, r0:r1, :] = jnp.broadcast_to(gate, (r1 - r0, gate_ref.shape[3]))


def _gather(slot, aff, f, L, caps):
    B, E, T = slot.shape
    D = f.shape[2]
    R = caps[0] + caps[1]
    return pl.pallas_call(
        functools.partial(_gather_kernel, L=L, caps=caps),
        out_shape=(jax.ShapeDtypeStruct((B, E, R, D), BF16), jax.ShapeDtypeStruct((B, E, R, LANES_V7X), F32)),
        grid=(B, E),
        in_specs=[pl.BlockSpec((1, 1, 1, T), lambda b, e: (b, e, 0, 0)),
                  pl.BlockSpec((1, 1, 1, T), lambda b, e: (b, e, 0, 0)),
                  pl.BlockSpec((1, T, D), lambda b, e: (b, 0, 0))],
        out_specs=(pl.BlockSpec((1, 1, R, D), lambda b, e: (b, e, 0, 0)),
                   pl.BlockSpec((1, 1, R, LANES_V7X), lambda b, e: (b, e, 0, 0))),
        compiler_params=_cparams(("parallel", "arbitrary")),
        name="moe_gather",
    )(slot.reshape(B, E, 1, T), aff.reshape(B, E, 1, T), f)


def _ffn_kernel(x_ref, gate_ref, w1_hbm, w3_hbm, w2_hbm, y_ref, w1b, w3b, w2b, st1, st3, st2, sems, *, layer, nk):
    e = pl.program_id(0)
    k = pl.program_id(1)
    has_next = e + 1 < pl.num_programs(0)
    bs, _, R, D = x_ref.shape
    r13 = w1b.shape[1] // nk
    r2 = w2b.shape[1] // nk

    def copies(ee, kk):
        return (pltpu.make_async_copy(w1_hbm.at[layer, ee, pl.ds(kk * r13, r13), :], st1, sems.at[0]),
                pltpu.make_async_copy(w3_hbm.at[layer, ee, pl.ds(kk * r13, r13), :], st3, sems.at[1]),
                pltpu.make_async_copy(w2_hbm.at[layer, ee, pl.ds(kk * r2, r2), :], st2, sems.at[2]))

    def land(slot, kk):
        w1b[slot, pl.ds(kk * r13, r13), :] = st1[...].astype(BF16)
        w3b[slot, pl.ds(kk * r13, r13), :] = st3[...].astype(BF16)
        w2b[slot, pl.ds(kk * r2, r2), :] = st2[...].astype(BF16)

    @pl.when((e == 0) & (k == 0))
    def _():
        for kk in range(nk):
            cs = copies(0, kk)
            for c in cs:
                c.start()
            for c in cs:
                c.wait()
            land(0, kk)

    @pl.when(has_next)
    def _():
        for c in copies(e + 1, k):
            c.start()

    slot = e % 2
    x = x_ref[...].reshape(bs * R, D)
    u = _dot(x, w1b[slot])
    g = _dot(x, w3b[slot])
    hm = (u * jax.nn.sigmoid(u) * g).astype(BF16)
    gate = gate_ref[...].reshape(bs * R, gate_ref.shape[3])[:, 0:1]
    y_ref[...] = (_dot(hm, w2b[slot]) * gate).astype(BF16).reshape(bs, 1, R, D)

    @pl.when(has_next)
    def _():
        for c in copies(e + 1, k):
            c.wait()
        land((e + 1) % 2, k)


def _ffn(xg, gate, w1, w3, w2, layer):
    B, E, R, D = xg.shape
    FF = w1.shape[3]
    bs = _pick(B, (2, 1))
    nk = B // bs
    assert D % nk == 0 and FF % nk == 0
    return pl.pallas_call(
        functools.partial(_ffn_kernel, layer=layer, nk=nk),
        out_shape=jax.ShapeDtypeStruct((B, E, R, D), BF16),
        grid=(E, nk),
        in_specs=[pl.BlockSpec((bs, 1, R, D), lambda e, b: (b, e, 0, 0)),
                  pl.BlockSpec((bs, 1, R, LANES_V7X), lambda e, b: (b, e, 0, 0)),
                  pl.BlockSpec(memory_space=pl.ANY),
                  pl.BlockSpec(memory_space=pl.ANY),
                  pl.BlockSpec(memory_space=pl.ANY)],
        out_specs=pl.BlockSpec((bs, 1, R, D), lambda e, b: (b, e, 0, 0)),
        scratch_shapes=[pltpu.VMEM((2, D, FF), BF16), pltpu.VMEM((2, D, FF), BF16), pltpu.VMEM((2, FF, D), BF16),
                        pltpu.VMEM((D // nk, FF), F32), pltpu.VMEM((D // nk, FF), F32), pltpu.VMEM((FF // nk, D), F32),
                        pltpu.SemaphoreType.DMA((3,))],
        compiler_params=_cparams(("arbitrary", "arbitrary")),
        name="moe_ffn",
    )(xg, gate, w1, w3, w2)


def _combine_kernel(slotc_ref, y_ref, h1_ref, mc_ref, ml_ref, o_ref, *, L, tt, caps):
    E = y_ref.shape[1]
    capc, capl = caps
    i = pl.program_id(2)
    row0 = i * tt
    sc = slotc_ref[0]
    g2 = _mod_rows(mc_ref, ml_ref, row0, tt, L, 5)

    @pl.when(row0 >= L)
    def _():
        jl = lax.broadcasted_iota(jnp.int32, (1, capl), 1).astype(F32)
        hit = jnp.concatenate([jnp.where(sc[:, e:e + 1] == jl, 1.0, 0.0).astype(BF16) for e in range(E)], axis=1)
        yl = y_ref[0, :, 0:capl, :].reshape(E * capl, y_ref.shape[3])
        o_ref[0] = h1_ref[0] + g2 * _dot(hit, yl)

    @pl.when(row0 < L)
    def _():
        jc = lax.broadcasted_iota(jnp.int32, (1, E * capc), 1).astype(F32)
        hit = jnp.zeros((tt, E * capc), F32)
        for e in range(E):
            se = sc[:, e:e + 1]
            hit = hit + jnp.where(jnp.where(se >= 0, se + (e * capc - capl), -1.0) == jc, 1.0, 0.0)
        yc = y_ref[0, :, capl:capl + capc, :].reshape(E * capc, y_ref.shape[3])
        o_ref[0] = h1_ref[0] + g2 * _dot(hit.astype(BF16), yc)


def _combine(slot_c, yff, h1, mc, ml, L, caps):
    B, T, D = h1.shape
    E, R = yff.shape[1], yff.shape[2]
    tt = _pick(L, (256, 128, 64))
    td = _pick(D, (1024, 512, 256, 128))
    return pl.pallas_call(
        functools.partial(_combine_kernel, L=L, tt=tt, caps=caps),
        out_shape=jax.ShapeDtypeStruct((B, T, D), F32),
        grid=(B, D // td, T // tt),
        in_specs=[pl.BlockSpec((1, tt, E), lambda b, j, i: (b, i, 0)),
                  pl.BlockSpec((1, E, R, td), lambda b, j, i: (b, 0, 0, j)),
                  pl.BlockSpec((1, tt, td), lambda b, j, i: (b, i, j)),
                  pl.BlockSpec((6, td), lambda b, j, i: (0, j)),
                  pl.BlockSpec((1, 6, td), lambda b, j, i: (b, 0, j))],
        out_specs=pl.BlockSpec((1, tt, td), lambda b, j, i: (b, i, j)),
        compiler_params=_cparams(("parallel", "parallel", "arbitrary")),
        name="moe_combine",
    )(slot_c, yff, h1, mc, ml)


def _moe(f, logits, h1, mc, ml, w1, w3, w2, layer, L):
    B, T, D = h1.shape
    caps = tuple(max(1, EC_CAPACITY * n // N_EXPERTS) for n in (L, T - L))
    slot, aff = _route(jnp.swapaxes(logits, 1, 2), L, caps)
    xg, gate = _gather(slot, aff, f, L, caps)
    yff = _ffn(xg, gate, w1, w3, w2, layer)
    slot_c = jnp.swapaxes(slot, 1, 2).astype(F32)
    return _combine(slot_c, yff, h1, mc, ml, L, caps)


def _final_kernel(h_ref, g_ref, o_ref):
    h = h_ref[0]
    o_ref[0] = h * lax.rsqrt(jnp.mean(h * h, axis=-1, keepdims=True) + EPS) * g_ref[...]


def _final_norm(h, g, L):
    B, T, D = h.shape
    tr = _pick(L, (256, 128, 64))
    off = L // tr
    return pl.pallas_call(
        _final_kernel,
        out_shape=jax.ShapeDtypeStruct((B, T - L, D), F32),
        grid=(B, (T - L) // tr),
        in_specs=[pl.BlockSpec((1, tr, D), lambda b, i: (b, i + off, 0)),
                  pl.BlockSpec((1, D), lambda b, i: (0, 0))],
        out_specs=pl.BlockSpec((1, tr, D), lambda b, i: (b, i, 0)),
        compiler_params=_cparams(("parallel", "parallel")),
        name="final_norm",
    )(h, g.reshape(1, D))


def kernel(x, c, ctx, c_ctx, ada_w, ada_b, norm_g, final_g, ret_wq, ret_wk, ret_wv, ret_wg, ret_wo, ret_decay, ret_gn_w, ret_gn_b, ml_wq, ml_wk, ml_wv, ml_wog, ml_wgate, ml_bgate, ml_norm_w, ml_wout, na_wqkv, na_rpb, na_wo, hg_wq, hg_wi, hg_wf, hg_wg, hg_norm_w, hg_wo, hg_lb, moe_router, moe_w1, moe_w3, moe_w2):
    B, N, D = x.shape
    L = ctx.shape[1]
    depth = ada_w.shape[0]
    h = jnp.concatenate([ctx, x], axis=1)
    mod_lat, mod_ctx = _mod_vectors(c, c_ctx, ada_w, ada_b)
    lb_all = jnp.cumsum(jax.nn.softmax(hg_lb.astype(F32), axis=0), axis=0)
    zero_b = lambda n: jnp.zeros((1, n), F32)
    gpad = LANES_V7X - 4 * ML_HEADS
    for i in range(depth):
        mc, ml = mod_ctx[i], mod_lat[i]
        kind = i % N_MIXERS
        if kind == 0:
            w = jnp.concatenate([ret_wq, ret_wk, ret_wv, ret_wg], 1).astype(BF16)
            y = _proj(h, norm_g[i, 0], mc, ml, w, zero_b(w.shape[1]), L)
            u = _retention_core(y, ret_decay, ret_gn_w, ret_gn_b, L)
            wo = ret_wo
        elif kind == 1:
            w = jnp.concatenate([ml_wq, ml_wk, ml_wv, ml_wog, ml_wgate[0], ml_wgate[1], jnp.zeros((D, gpad), F32)],
                                1).astype(BF16)
            bias = jnp.concatenate([jnp.zeros((3 * D,), F32), ml_bgate[0], ml_bgate[1], jnp.zeros((gpad,), F32)])[None]
            y = _proj(h, norm_g[i, 0], mc, ml, w, bias, L)
            u = _mlstm_core(y, ml_norm_w, D, L)
            wo = ml_wout
        elif kind == 2:
            w = na_wqkv.astype(BF16)
            y = _proj(h, norm_g[i, 0], mc, ml, w, zero_b(w.shape[1]), L, out_dtype=BF16)
            u = _na_core(y, na_rpb, L)
            wo = na_wo
        else:
            w = jnp.concatenate([hg_wq, hg_wi, hg_wf[0], hg_wf[1], hg_wg], 1).astype(BF16)
            y = _proj(h, norm_g[i, 0], mc, ml, w, zero_b(w.shape[1]), L)
            u = _hgrn2_core(y, lb_all[i] - lb_all[0], hg_norm_w, L)
            wo = hg_wo
        h1, f, logits = _outproj(u, wo.astype(BF16), h, norm_g[i, 1], mc, ml, moe_router[i], L)
        h = _moe(f, logits, h1, mc, ml, moe_w1, moe_w3, moe_w2, i, L)
    return _final_norm(h, final_g, L)
```

```python
import functools

import numpy as np
import jax
import jax.numpy as jnp
from jax import lax
from jax.experimental import pallas as pl
from jax.experimental.pallas import tpu as pltpu

GRID_W = 64
EPS = 1e-6
NEG = -1e30
CHUNK = 64
ROPE_THETA = 10000.0
RET_HEADS = 8
ML_HEADS = 8
NA_HEADS = 16
NA_WIN_R = 8
NA_WIN_C = 16
HG_HEADS = 16
N_EXPERTS = 16
EC_CAPACITY = 2
N_MIXERS = 4

VMEM_LIMIT_BYTES_V7X = 58 * 1024 * 1024
LANES_V7X = 128

F32 = jnp.float32
BF16 = jnp.bfloat16


def _cparams(sem):
    return pltpu.CompilerParams(dimension_semantics=sem, vmem_limit_bytes=VMEM_LIMIT_BYTES_V7X)


def _pick(n, prefs):
    for p in prefs:
        if n % p == 0:
            return p
    return n


def _dot(a, b):
    return jnp.dot(a, b, preferred_element_type=F32)


def _dot_nt(a, b):
    return lax.dot_general(a, b, (((1,), (1,)), ((), ())), preferred_element_type=F32)


def _dot_tn(a, b):
    return lax.dot_general(a, b, (((0,), (0,)), ((), ())), preferred_element_type=F32)


def _norm_mod(h, g, sh, sc):
    ms = jnp.mean(h * h, axis=-1, keepdims=True)
    y = h * lax.rsqrt(ms + EPS) * g
    return y * (1.0 + sc) + sh


def _mod_rows(mc_ref, ml_ref, row0, tm, L, k):
    row = row0 + lax.broadcasted_iota(jnp.int32, (tm, 1), 0)
    return jnp.where(row < L, mc_ref[k:k + 1, :], ml_ref[0, k:k + 1, :])


def _seq_block(T, L):
    cb = _pick(L, (256, 128, 64))
    assert L % cb == 0 and (T - L) % cb == 0
    return cb


def _bwd_block(s, nctx, ntot):
    return nctx - 1 - s if s < nctx else ntot - 1 - (s - nctx)


def _rows(i, n):
    return slice(i * n, (i + 1) * n)


def _mod_kernel(c_ref, w_ref, b_ref, o_ref):
    c = c_ref[...]
    s = (c * jax.nn.sigmoid(c)).astype(BF16)
    o_ref[0] = _dot(s, w_ref[0].astype(BF16)) + b_ref[0]


def _mod_vectors(c, c_ctx, ada_w, ada_b):
    B, D = c.shape
    depth, _, n6 = ada_w.shape
    mp = -(-(B + 1) // 8) * 8
    c_all = jnp.concatenate([c, c_ctx[None], jnp.zeros((mp - B - 1, D), F32)], 0)
    tn = _pick(n6, (1024, 512, 256, 128))
    out = pl.pallas_call(
        _mod_kernel,
        out_shape=jax.ShapeDtypeStruct((depth, mp, n6), F32),
        grid=(depth, n6 // tn),
        in_specs=[pl.BlockSpec((mp, D), lambda i, j: (0, 0)),
                  pl.BlockSpec((1, D, tn), lambda i, j: (i, 0, j)),
                  pl.BlockSpec((1, 1, tn), lambda i, j: (i, 0, j))],
        out_specs=pl.BlockSpec((1, mp, tn), lambda i, j: (i, 0, j)),
        compiler_params=_cparams(("parallel", "parallel")),
        name="mod_vectors",
    )(c_all, ada_w, ada_b.reshape(depth, 1, n6))
    mod_lat = out[:, :B].reshape(depth, B, 6, D)
    mod_ctx = out[:, B].reshape(depth, 6, D)
    return mod_lat, mod_ctx


def _proj_kernel(h_ref, g_ref, mc_ref, ml_ref, w_ref, b_ref, o_ref, a_scr, *, L, tm, ts):
    i = pl.program_id(1)
    j = pl.program_id(2)

    @pl.when(j == 0)
    def _():
        for sub in range(tm // ts):
            rs = slice(sub * ts, (sub + 1) * ts)
            sh = _mod_rows(mc_ref, ml_ref, i * tm + sub * ts, ts, L, 0)
            sc = _mod_rows(mc_ref, ml_ref, i * tm + sub * ts, ts, L, 1)
            a = _norm_mod(h_ref[0, rs, :], g_ref[...], sh, sc).astype(BF16)
            a_scr[rs, :] = a
            o_ref[0, rs, :] = (_dot(a, w_ref[...]) + b_ref[...]).astype(o_ref.dtype)

    @pl.when(j > 0)
    def _():
        o_ref[0] = (_dot(a_scr[...], w_ref[...]) + b_ref[...]).astype(o_ref.dtype)


def _proj(h, g, mc, ml, w, bias, L, out_dtype=F32):
    B, T, D = h.shape
    n = w.shape[1]
    tm = _pick(T, (1152, 768, 576, 384, 256, 128, 64))
    ts = _pick(tm, (128, 64))
    tn = _pick(n, (1024, 896, 512, 256, 128))
    return pl.pallas_call(
        functools.partial(_proj_kernel, L=L, tm=tm, ts=ts),
        out_shape=jax.ShapeDtypeStruct((B, T, n), out_dtype),
        grid=(B, T // tm, n // tn),
        in_specs=[pl.BlockSpec((1, tm, D), lambda b, i, j: (b, i, 0)),
                  pl.BlockSpec((1, D), lambda b, i, j: (0, 0)),
                  pl.BlockSpec((6, D), lambda b, i, j: (0, 0)),
                  pl.BlockSpec((1, 6, D), lambda b, i, j: (b, 0, 0)),
                  pl.BlockSpec((D, tn), lambda b, i, j: (0, j)),
                  pl.BlockSpec((1, tn), lambda b, i, j: (0, j))],
        out_specs=pl.BlockSpec((1, tm, tn), lambda b, i, j: (b, i, j)),
        scratch_shapes=[pltpu.VMEM((tm, D), BF16)],
        compiler_params=_cparams(("parallel", "parallel", "arbitrary")),
        name="proj",
    )(h, g.reshape(1, D), mc, ml, w, bias)


def _head_norm(o, w, b, center):
    if center:
        o = o - jnp.mean(o, axis=-1, keepdims=True)
    o = o * lax.rsqrt(jnp.mean(o * o, axis=-1, keepdims=True) + EPS)
    o = o * w
    if b is not None:
        o = o + b
    return o


def _ret_kernel(dec_ref, q_ref, k_ref, v_ref, g_ref, cos_ref, sin_ref, gw_ref, gb_ref, u_ref,
                s_scr, dm_scr, of_scr, ob_scr, *, dk, C, nctx, ntot, scale):
    h = pl.program_id(1)
    half = dk // 2
    ri = lax.broadcasted_iota(jnp.int32, (C, C), 0)
    ci = lax.broadcasted_iota(jnp.int32, (C, C), 1)
    dif = (ri - ci).astype(F32)
    pos = lax.broadcasted_iota(jnp.int32, (C, 1), 0).astype(F32)
    lg_f = jax.nn.log_sigmoid(jnp.full((1, 1), dec_ref[0, h], F32))
    lg_b = jax.nn.log_sigmoid(jnp.full((1, 1), dec_ref[1, h], F32))
    dm_scr[0] = jnp.where(ri >= ci, jnp.exp(lg_f * dif), 0.0)
    dm_scr[1] = jnp.where(ri <= ci, jnp.exp(-lg_b * dif), 0.0)
    q_dec = (jnp.exp(lg_f * (pos + 1.0)), jnp.exp(lg_b * (C - pos)))
    k_dec = (jnp.exp(lg_f * (C - 1.0 - pos)), jnp.exp(lg_b * pos))
    s_dec = (jnp.exp(lg_f * C), jnp.exp(lg_b * C))
    s_scr[...] = jnp.zeros(s_scr.shape, F32)

    def step(rows, d):
        cos = cos_ref[rows, :]
        sin = sin_ref[rows, :]
        q = q_ref[0, rows, :]
        k = k_ref[0, rows, :]
        q1, q2 = q[:, :half], q[:, half:]
        k1, k2 = k[:, :half], k[:, half:]
        q = jnp.concatenate([q1 * cos - q2 * sin, q1 * sin + q2 * cos], axis=1)
        k = jnp.concatenate([k1 * cos - k2 * sin, k1 * sin + k2 * cos], axis=1) * scale
        vb = v_ref[0, rows, :].astype(BF16)
        a = _dot_nt(q.astype(BF16), k.astype(BF16)) * dm_scr[d]
        st = s_scr[d]
        o = _dot(a.astype(BF16), vb) + _dot_nt((q * q_dec[d]).astype(BF16), st.astype(BF16))
        s_scr[d] = s_dec[d] * st + _dot_tn(vb, (k * k_dec[d]).astype(BF16))
        return o

    for s in range(ntot):
        rf = _rows(s, C)
        of_scr[rf, :] = step(rf, 0)
        rb = _rows(_bwd_block(s, nctx, ntot), C)
        ob_scr[rb, :] = step(rb, 1)

    for i in range(ntot):
        rows = _rows(i, C)
        o = _head_norm(of_scr[rows, :] + ob_scr[rows, :], gw_ref[...], gb_ref[...], True)
        g = g_ref[0, rows, :]
        u_ref[0, rows, :] = (g * jax.nn.sigmoid(g) * o).astype(BF16)


def _rope_tables(N, L, dk):
    t = np.arange(N)
    row = (t // GRID_W).astype(np.float32)
    col = (t % GRID_W).astype(np.float32)
    quarter = dk // 4
    inv = (ROPE_THETA ** (-np.arange(quarter, dtype=np.float32) / quarter)).astype(np.float32)
    ang = np.concatenate([row[:, None] * inv, col[:, None] * inv], -1).astype(np.float32)
    cos = np.concatenate([np.ones((L, dk // 2), np.float32), np.cos(ang)], 0)
    sin = np.concatenate([np.zeros((L, dk // 2), np.float32), np.sin(ang)], 0)
    return jnp.asarray(cos, F32), jnp.asarray(sin, F32)


def _retention_core(y, decay, gn_w, gn_b, L):
    B, T, n4 = y.shape
    D = n4 // 4
    H = RET_HEADS
    dk = D // H
    dv = dk
    C = _seq_block(T, L)
    cos, sin = _rope_tables(T - L, L, dk)
    blk = lambda seg: pl.BlockSpec((1, T, dk), lambda b, h, seg=seg: (b, 0, seg * H + h))
    return pl.pallas_call(
        functools.partial(_ret_kernel, dk=dk, C=C, nctx=L // C, ntot=T // C, scale=dk ** -0.5),
        out_shape=jax.ShapeDtypeStruct((B, T, D), BF16),
        grid=(B, H),
        in_specs=[pl.BlockSpec(memory_space=pltpu.SMEM),
                  blk(0), blk(1), blk(2), blk(3),
                  pl.BlockSpec((T, dk // 2), lambda b, h: (0, 0)),
                  pl.BlockSpec((T, dk // 2), lambda b, h: (0, 0)),
                  pl.BlockSpec((1, dv), lambda b, h: (0, h)),
                  pl.BlockSpec((1, dv), lambda b, h: (0, h))],
        out_specs=pl.BlockSpec((1, T, dv), lambda b, h: (b, 0, h)),
        scratch_shapes=[pltpu.VMEM((2, dv, dk), F32), pltpu.VMEM((2, C, C), F32),
                        pltpu.VMEM((T, dv), F32), pltpu.VMEM((T, dv), F32)],
        compiler_params=_cparams(("parallel", "parallel")),
        name="retention_core",
    )(decay.astype(F32), y, y, y, y, cos, sin, gn_w.reshape(1, D), gn_b.reshape(1, D))


def _chunk_cumsum(g, C, reverse):
    n = g.shape[0]
    pos = lax.broadcasted_iota(jnp.int32, (n, 1), 0) & (C - 1)
    x = g
    s = 1
    while s < C:
        if reverse:
            x = x + jnp.where(pos < C - s, pltpu.roll(x, n - s, axis=0), 0.0)
        else:
            x = x + jnp.where(pos >= s, pltpu.roll(x, s, axis=0), 0.0)
        s *= 2
    return x


def _hg_kernel(q_ref, i_ref, zf_ref, zb_ref, g_ref, lb_ref, nw_ref, u_ref,
               s_scr, of_scr, ob_scr, *, hp, dk, dv, C, CB, nctx, ntot, scale):
    nb = CB // C
    sh = C.bit_length() - 1
    ri = lax.broadcasted_iota(jnp.int32, (CB, CB), 0)
    ci = lax.broadcasted_iota(jnp.int32, (CB, CB), 1)
    same = (ri >> sh) == (ci >> sh)
    m_lo = same & (ri >= ci)
    m_up = same & (ri <= ci)
    s_scr[...] = jnp.zeros(s_scr.shape, F32)

    def block(rows, hh, z_ref, mask, reverse, o_scr, idx):
        cs = slice(hh * dk, (hh + 1) * dk)
        vs = slice(hh * dv, (hh + 1) * dv)
        lb = lb_ref[:, cs]
        qr = q_ref[0, rows, cs]
        q = qr * jax.nn.sigmoid(qr) * scale
        z = z_ref[0, rows, cs]
        sig = jax.nn.sigmoid(z)
        f = lb + (1.0 - lb) * sig
        k = (1.0 - lb) * (1.0 - sig)
        vb = i_ref[0, rows, vs].astype(BF16)
        b = _chunk_cumsum(jnp.log(f), C, reverse)
        b3 = b.reshape(nb, C, dk)
        e = 0 if reverse else C - 1
        b_end3 = b3[:, e:e + 1, :]
        q_in = (q * jnp.exp(b)).astype(BF16)
        k_in = (k * jnp.exp(-b)).astype(BF16)
        k_end = (k.reshape(nb, C, dk) * jnp.exp(b_end3 - b3)).reshape(CB, dk).astype(BF16)
        a = jnp.where(mask, _dot_nt(q_in, k_in), 0.0)
        o = _dot(a.astype(BF16), vb)
        st = s_scr[idx]
        inter = [None] * nb
        for c in (range(nb - 1, -1, -1) if reverse else range(nb)):
            rc = slice(c * C, (c + 1) * C)
            inter[c] = _dot_nt(q_in[rc], st.astype(BF16))
            st = jnp.exp(b_end3[c]) * st + _dot_tn(vb[rc], k_end[rc])
        s_scr[idx] = st
        o_scr[rows, vs] = o + jnp.concatenate(inter, axis=0)

    for s in range(ntot):
        rf = _rows(s, CB)
        rb = _rows(_bwd_block(s, nctx, ntot), CB)
        for hh in range(hp):
            block(rf, hh, zf_ref, m_lo, False, of_scr, 2 * hh)
            block(rb, hh, zb_ref, m_up, True, ob_scr, 2 * hh + 1)

    for i in range(ntot):
        rows = _rows(i, CB)
        for hh in range(hp):
            cs = slice(hh * dv, (hh + 1) * dv)
            o = _head_norm(of_scr[rows, cs] + ob_scr[rows, cs], nw_ref[:, cs], None, False)
            g = g_ref[0, rows, cs]
            u_ref[0, rows, cs] = (g * jax.nn.sigmoid(g) * o).astype(BF16)


def _hgrn2_core(y, lb, norm_w, L):
    B, T, n5 = y.shape
    D = n5 // 5
    H = HG_HEADS
    dk = D // H
    dv = dk
    hp = 2 if H % 2 == 0 else 1
    G = H // hp
    CB = _seq_block(T, L)
    C = min(CHUNK, CB)
    blk = lambda seg: pl.BlockSpec((1, T, hp * dk), lambda b, h, seg=seg: (b, 0, seg * G + h))
    return pl.pallas_call(
        functools.partial(_hg_kernel, hp=hp, dk=dk, dv=dv, C=C, CB=CB, nctx=L // CB, ntot=T // CB, scale=dk ** -0.5),
        out_shape=jax.ShapeDtypeStruct((B, T, D), BF16),
        grid=(B, G),
        in_specs=[blk(0), blk(1), blk(2), blk(3), blk(4),
                  pl.BlockSpec((1, hp * dk), lambda b, h: (0, h)),
                  pl.BlockSpec((1, hp * dv), lambda b, h: (0, h))],
        out_specs=pl.BlockSpec((1, T, hp * dv), lambda b, h: (b, 0, h)),
        scratch_shapes=[pltpu.VMEM((2 * hp, dv, dk), F32), pltpu.VMEM((T, hp * dv), F32),
                        pltpu.VMEM((T, hp * dv), F32)],
        compiler_params=_cparams(("parallel", "parallel")),
        name="hgrn2_core",
    )(y, y, y, y, y, lb.reshape(1, D).astype(F32), norm_w.reshape(1, D))


def _tri(C):
    r = lax.broadcasted_iota(jnp.int32, (C, C), 0)
    c = lax.broadcasted_iota(jnp.int32, (C, C), 1)
    return r >= c, r <= c


def _ml_kernel(q_ref, k_ref, v_ref, og_ref, gt_ref, nw_ref, u_ref, c_scr, n_scr, m_scr, of_scr, ob_scr,
               *, H, C, nctx, ntot, scale):
    h = pl.program_id(1)
    lower, upper = _tri(C)
    c_scr[...] = jnp.zeros(c_scr.shape, F32)
    n_scr[...] = jnp.zeros(n_scr.shape, F32)
    m_scr[...] = jnp.zeros(m_scr.shape, F32)
    gw = gt_ref.shape[2]
    lane = lax.broadcasted_iota(jnp.int32, (C, gw), 1)

    def chunk(rows, d, mask, end_row, o_scr):
        gt = gt_ref[0, rows, :]
        cum = _chunk_cumsum(jax.nn.log_sigmoid(gt), C, d == 1)
        b_col = jnp.sum(jnp.where(lane == h + H + 2 * H * d, cum, 0.0), axis=1, keepdims=True)
        i_col = jnp.sum(jnp.where(lane == h + 2 * H * d, gt, 0.0), axis=1, keepdims=True)
        z = jnp.where(lane == 0, b_col, jnp.where(lane == 1, i_col, 0.0))
        r = jnp.transpose(z)
        b_row, i_row = r[0:1, :], r[1:2, :]
        q = q_ref[0, rows, :]
        k = k_ref[0, rows, :] * scale
        v = v_ref[0, rows, :]
        m_prev = m_scr[d, 0:1, 0:1]
        logw = jnp.where(mask, b_col - b_row + i_row, -jnp.inf)
        log_carry = b_col + m_prev
        m_i = jnp.maximum(log_carry, jnp.max(logw, axis=1, keepdims=True))
        qb, vb = q.astype(BF16), v.astype(BF16)
        s = _dot_nt(qb, k.astype(BF16)) * jnp.exp(logw - m_i)
        a = jnp.exp(log_carry - m_i)
        cst = c_scr[d]
        nst = n_scr[d]
        num = _dot(s.astype(BF16), vb) + a * _dot(qb, cst.astype(BF16))
        den = jnp.sum(s, axis=1, keepdims=True) + a * jnp.sum(q * nst, axis=1, keepdims=True)
        o_scr[rows, :] = num / jnp.maximum(jnp.abs(den), jnp.exp(-m_i))
        b_end = b_col[end_row:end_row + 1, :]
        logw_end = b_end - b_col + i_col
        m_new = jnp.maximum(b_end + m_prev, jnp.max(logw_end, axis=0, keepdims=True))
        decay = jnp.exp(b_end + m_prev - m_new)
        kw = k * jnp.exp(logw_end - m_new)
        c_scr[d] = decay * cst + _dot_tn(kw.astype(BF16), vb)
        n_scr[d] = decay * nst + jnp.sum(kw, axis=0, keepdims=True)
        m_scr[d] = jnp.broadcast_to(m_new, m_scr.shape[1:])

    for s in range(ntot):
        chunk(_rows(s, C), 0, lower, C - 1, of_scr)
        chunk(_rows(_bwd_block(s, nctx, ntot), C), 1, upper, 0, ob_scr)

    for i in range(ntot):
        rows = _rows(i, C)
        o = _head_norm(of_scr[rows, :] + ob_scr[rows, :], nw_ref[...], None, True)
        u_ref[0, rows, :] = (jax.nn.sigmoid(og_ref[0, rows, :]) * o).astype(BF16)


def _mlstm_core(y, norm_w, D, L):
    B, T, n = y.shape
    H = ML_HEADS
    dv = D // H
    dk = dv // 2
    gw = n - 3 * D
    C = _seq_block(T, L)
    return pl.pallas_call(
        functools.partial(_ml_kernel, H=H, C=C, nctx=L // C, ntot=T // C, scale=dk ** -0.5),
        out_shape=jax.ShapeDtypeStruct((B, T, D), BF16),
        grid=(B, H),
        in_specs=[pl.BlockSpec((1, T, dk), lambda b, h: (b, 0, h)),
                  pl.BlockSpec((1, T, dk), lambda b, h: (b, 0, H + h)),
                  pl.BlockSpec((1, T, dv), lambda b, h: (b, 0, H + h)),
                  pl.BlockSpec((1, T, dv), lambda b, h: (b, 0, 2 * H + h)),
                  pl.BlockSpec((1, T, gw), lambda b, h: (b, 0, 3 * D // gw)),
                  pl.BlockSpec((1, dv), lambda b, h: (0, h))],
        out_specs=pl.BlockSpec((1, T, dv), lambda b, h: (b, 0, h)),
        scratch_shapes=[pltpu.VMEM((2, dk, dv), F32), pltpu.VMEM((2, 1, dk), F32), pltpu.VMEM((2, 8, LANES_V7X), F32),
                        pltpu.VMEM((T, dv), F32), pltpu.VMEM((T, dv), F32)],
        compiler_params=_cparams(("parallel", "parallel")),
        name="mlstm_core",
    )(y, y, y, y, y, norm_w.reshape(1, D))


def _na_kernel(q_ref, k_ref, v_ref, toe_ref, o_ref, bias_scr, *, L, rows, wr, G, kr, layouts, cls, scale):
    W = GRID_W

    @pl.when(pl.program_id(1) == 0)
    def _():
        for l, lay in enumerate(layouts):
            for rr, row in enumerate(lay):
                for kk, d in enumerate(row):
                    blk = toe_ref[0, d] if d >= 0 else jnp.full((W, W), NEG, F32)
                    bias_scr[l, rr * W:(rr + 1) * W, kk * W:(kk + 1) * W] = blk

    kc = k_ref[0, 0:L, :].astype(BF16)
    vc = v_ref[0, 0:L, :].astype(BF16)
    qc = q_ref[0, 0:L, :].astype(BF16)
    s = _dot_nt(qc, kc) * scale
    p = jnp.exp(s - jnp.max(s, axis=1, keepdims=True))
    o_ref[0, 0:L, :] = (_dot(p.astype(BF16), vc) / jnp.sum(p, axis=1, keepdims=True)).astype(BF16)

    for g in range(rows // G):
        base = min(max(g * G - wr // 2, 0), rows - kr)
        qrows = slice(L + g * G * W, L + (g + 1) * G * W)
        krows = slice(L + base * W, L + (base + kr) * W)
        q = q_ref[0, qrows, :].astype(BF16)
        kw = k_ref[0, krows, :].astype(BF16)
        vw = v_ref[0, krows, :].astype(BF16)
        s_win = _dot_nt(q, kw) * scale + bias_scr[cls[g]]
        s_ctx = _dot_nt(q, kc) * scale
        m = jnp.maximum(jnp.max(s_win, axis=1, keepdims=True), jnp.max(s_ctx, axis=1, keepdims=True))
        p_win = jnp.exp(s_win - m)
        p_ctx = jnp.exp(s_ctx - m)
        den = jnp.sum(p_win, axis=1, keepdims=True) + jnp.sum(p_ctx, axis=1, keepdims=True)
        o = _dot(p_win.astype(BF16), vw) + _dot(p_ctx.astype(BF16), vc)
        o_ref[0, qrows, :] = (o / den).astype(BF16)


def _na_groups(rows):
    wr = min(NA_WIN_R, rows)
    G = 4 if rows % 4 == 0 else 1
    kr = min(rows, wr + G - 1)
    layouts, cls = [], []
    for g in range(rows // G):
        base = min(max(g * G - wr // 2, 0), rows - kr)
        lay = []
        for rr in range(G):
            r = g * G + rr
            r0 = min(max(r - wr // 2, 0), rows - wr)
            lay.append(tuple((base + kk - r + NA_WIN_R - 1) if r0 <= base + kk < r0 + wr else -1 for kk in range(kr)))
        lay = tuple(lay)
        if lay not in layouts:
            layouts.append(lay)
        cls.append(layouts.index(lay))
    return G, kr, wr, tuple(layouts), tuple(cls)


def _na_rel_blocks(rpb):
    H = rpb.shape[0]
    W = GRID_W
    pad = W - NA_WIN_C
    rp = rpb.astype(F32)
    wide = jnp.concatenate([jnp.repeat(rp[..., :1], pad, axis=-1), rp, jnp.repeat(rp[..., -1:], pad, axis=-1)], -1)
    qi, kcol = np.arange(W)[:, None], np.arange(W)[None, :]
    onehot = (np.arange(2 * W - 1)[:, None, None] == (kcol - qi + W - 1)[None]).astype(np.float32)
    toe = jnp.dot(wide.reshape(-1, 2 * W - 1), jnp.asarray(onehot.reshape(2 * W - 1, W * W)),
                  precision=lax.Precision.HIGHEST).reshape(H, -1, W, W)
    c0 = np.clip(qi - NA_WIN_C // 2, 0, W - NA_WIN_C)
    ok = (kcol >= c0) & (kcol < c0 + NA_WIN_C)
    return jnp.where(jnp.asarray(ok), toe, NEG)


def _na_core(y, rpb, L):
    B, T, n3 = y.shape
    D = n3 // 3
    H = NA_HEADS
    dh = D // H
    rows = (T - L) // GRID_W
    G, kr, wr, layouts, cls = _na_groups(rows)
    toe = _na_rel_blocks(rpb)
    blk = lambda seg: pl.BlockSpec((1, T, dh), lambda h, b, seg=seg: (b, 0, seg * H + h))
    return pl.pallas_call(
        functools.partial(_na_kernel, L=L, rows=rows, wr=wr, G=G, kr=kr, layouts=layouts, cls=cls, scale=dh ** -0.5),
        out_shape=jax.ShapeDtypeStruct((B, T, D), BF16),
        grid=(H, B),
        in_specs=[blk(0), blk(1), blk(2),
                  pl.BlockSpec((1,) + toe.shape[1:], lambda h, b: (h, 0, 0, 0))],
        out_specs=pl.BlockSpec((1, T, dh), lambda h, b: (b, 0, h)),
        scratch_shapes=[pltpu.VMEM((len(layouts), G * GRID_W, kr * GRID_W), F32)],
        compiler_params=_cparams(("arbitrary", "arbitrary")),
        name="na_core",
    )(y, y, y, toe)


def _outproj_kernel(u_ref, wo_ref, h_ref, g_ref, mc_ref, ml_ref, wr_ref, h1_ref, f_ref, lg_ref, *, L, tm, nsub):
    E = lg_ref.shape[2]
    ts = tm // nsub
    for sub in range(nsub):
        rs = slice(sub * ts, (sub + 1) * ts)
        row0 = pl.program_id(1) * tm + sub * ts
        g1 = _mod_rows(mc_ref, ml_ref, row0, ts, L, 2)
        sh2 = _mod_rows(mc_ref, ml_ref, row0, ts, L, 3)
        sc2 = _mod_rows(mc_ref, ml_ref, row0, ts, L, 4)
        h1 = h_ref[0, rs, :] + g1 * _dot(u_ref[0, rs, :], wo_ref[...])
        h1_ref[0, rs, :] = h1
        f = _norm_mod(h1, g_ref[...], sh2, sc2)
        f_hi = f.astype(BF16)
        f_ref[0, rs, :] = f_hi
        r1 = _dot(f_hi, wr_ref[...])
        r2 = _dot((f - f_hi.astype(F32)).astype(BF16), wr_ref[...])
        lg_ref[0, rs, :] = r1[:, :E] + r1[:, E:] + r2[:, :E]


def _outproj(u, wo, h, g, mc, ml, w_router, L):
    B, T, D = h.shape
    E = w_router.shape[1]
    wr_hi = w_router.astype(BF16)
    wr_lo = (w_router - wr_hi.astype(F32)).astype(BF16)
    wr2 = jnp.concatenate([wr_hi, wr_lo], axis=1)
    tm = _pick(T, (384, 256, 128, 64))
    nsub = 2 if tm % 32 == 0 else 1
    return pl.pallas_call(
        functools.partial(_outproj_kernel, L=L, tm=tm, nsub=nsub),
        out_shape=(jax.ShapeDtypeStruct((B, T, D), F32), jax.ShapeDtypeStruct((B, T, D), BF16),
                   jax.ShapeDtypeStruct((B, T, E), F32)),
        grid=(B, T // tm),
        in_specs=[pl.BlockSpec((1, tm, D), lambda b, i: (b, i, 0)),
                  pl.BlockSpec((D, D), lambda b, i: (0, 0)),
                  pl.BlockSpec((1, tm, D), lambda b, i: (b, i, 0)),
                  pl.BlockSpec((1, D), lambda b, i: (0, 0)),
                  pl.BlockSpec((6, D), lambda b, i: (0, 0)),
                  pl.BlockSpec((1, 6, D), lambda b, i: (b, 0, 0)),
                  pl.BlockSpec((D, 2 * E), lambda b, i: (0, 0))],
        out_specs=(pl.BlockSpec((1, tm, D), lambda b, i: (b, i, 0)),
                   pl.BlockSpec((1, tm, D), lambda b, i: (b, i, 0)),
                   pl.BlockSpec((1, tm, E), lambda b, i: (b, i, 0))),
        compiler_params=_cparams(("parallel", "parallel")),
        name="outproj",
    )(u, wo, h, g.reshape(1, D), mc, ml, wr2)


def _route_kernel(lg_ref, slot_ref, aff_ref, before_scr, *, L, caps):
    x = lg_ref[...]
    B, E, T = x.shape
    e = jnp.exp(x - jnp.max(x, axis=1, keepdims=True))
    aff = e / jnp.sum(e, axis=1, keepdims=True)
    aff_ref[...] = aff
    nmax = before_scr.shape[0]
    rb = _pick(nmax, (256, 128, 64))

    def fill(i, carry):
        r = i * rb + lax.broadcasted_iota(jnp.int32, (rb, nmax), 0)
        before_scr[pl.ds(pl.multiple_of(i * rb, rb), rb), :] = jnp.where(
            r < lax.broadcasted_iota(jnp.int32, (rb, nmax), 1), 1.0, 0.0).astype(BF16)
        return carry

    lax.fori_loop(0, nmax // rb, fill, 0)
    base = 0
    for lo, hi, cap in ((L, T, caps[1]), (0, L, caps[0])):
        n = hi - lo
        before = before_scr[0:n, 0:n]
        a = aff[:, :, lo:hi].reshape(B * E, n)
        bits = lax.bitcast_convert_type(a, jnp.int32)

        def search(it, thr, bits=bits, cap=cap):
            cand = thr | lax.shift_left(jnp.int32(1), 30 - it)
            cnt = jnp.sum(jnp.where(bits >= cand, 1.0, 0.0), axis=1, keepdims=True)
            return jnp.where(cnt >= cap, cand, thr)

        thr = lax.fori_loop(0, 31, search, jnp.zeros((B * E, 1), jnp.int32))
        gt = bits > thr
        eq = bits == thr
        need = cap - jnp.sum(jnp.where(gt, 1.0, 0.0), axis=1, keepdims=True)
        rank_eq = _dot(jnp.where(eq, 1.0, 0.0).astype(BF16), before)
        sel = gt | (eq & (rank_eq < need))
        pos = _dot(jnp.where(sel, 1.0, 0.0).astype(BF16), before)
        slot = jnp.where(sel, pos.astype(jnp.int32) + base, -1)
        slot_ref[:, :, lo:hi] = slot.reshape(B, E, n)
        base += cap


def _route(logits_t, L, caps):
    B, E, T = logits_t.shape
    return pl.pallas_call(
        functools.partial(_route_kernel, L=L, caps=caps),
        out_shape=(jax.ShapeDtypeStruct((B, E, T), jnp.int32), jax.ShapeDtypeStruct((B, E, T), F32)),
        scratch_shapes=[pltpu.VMEM((max(L, T - L), max(L, T - L)), BF16)],
        compiler_params=pltpu.CompilerParams(vmem_limit_bytes=VMEM_LIMIT_BYTES_V7X),
        name="route",
    )(logits_t)


def _gather_kernel(slot_ref, aff_ref, f_ref, x_ref, gate_ref, *, L, caps):
    T = f_ref.shape[1]
    capc, capl = caps
    for lo, hi, r0, r1 in ((L, T, 0, capl), (0, L, capl, capl + capc)):
        hit = (lax.broadcasted_iota(jnp.int32, (r1 - r0, hi - lo), 0) + r0) == slot_ref[0, 0, :, lo:hi]
        x_ref[0, 0, r0:r1, :] = _dot(jnp.where(hit, 1.0, 0.0).astype(BF16), f_ref[0, lo:hi, :]).astype(BF16)
        gate = jnp.sum(jnp.where(hit, aff_ref[0, 0, :, lo:hi], 0.0), axis=1, keepdims=True)
        gate_ref[0, 0, r0:r1, :] = jnp.broadcast_to(gate, (r1 - r0, gate_ref.shape[3]))


def _gather(slot, aff, f, L, caps):
    B, E, T = slot.shape
    D = f.shape[2]
    R = caps[0] + caps[1]
    return pl.pallas_call(
        functools.partial(_gather_kernel, L=L, caps=caps),
        out_shape=(jax.ShapeDtypeStruct((B, E, R, D), BF16), jax.ShapeDtypeStruct((B, E, R, LANES_V7X), F32)),
        grid=(B, E),
        in_specs=[pl.BlockSpec((1, 1, 1, T), lambda b, e: (b, e, 0, 0)),
                  pl.BlockSpec((1, 1, 1, T), lambda b, e: (b, e, 0, 0)),
                  pl.BlockSpec((1, T, D), lambda b, e: (b, 0, 0))],
        out_specs=(pl.BlockSpec((1, 1, R, D), lambda b, e: (b, e, 0, 0)),
                   pl.BlockSpec((1, 1, R, LANES_V7X), lambda b, e: (b, e, 0, 0))),
        compiler_params=_cparams(("parallel", "arbitrary")),
        name="moe_gather",
    )(slot.reshape(B, E, 1, T), aff.reshape(B, E, 1, T), f)


def _ffn_kernel(x_ref, gate_ref, w1_hbm, w3_hbm, w2_hbm, y_ref, w1b, w3b, w2b, st1, st3, st2, sems, *, layer, nk):
    e = pl.program_id(0)
    k = pl.program_id(1)
    has_next = e + 1 < pl.num_programs(0)
    bs, _, R, D = x_ref.shape
    r13 = w1b.shape[1] // nk
    r2 = w2b.shape[1] // nk

    def copies(ee, kk):
        return (pltpu.make_async_copy(w1_hbm.at[layer, ee, pl.ds(kk * r13, r13), :], st1, sems.at[0]),
                pltpu.make_async_copy(w3_hbm.at[layer, ee, pl.ds(kk * r13, r13), :], st3, sems.at[1]),
                pltpu.make_async_copy(w2_hbm.at[layer, ee, pl.ds(kk * r2, r2), :], st2, sems.at[2]))

    def land(slot, kk):
        w1b[slot, pl.ds(kk * r13, r13), :] = st1[...].astype(BF16)
        w3b[slot, pl.ds(kk * r13, r13), :] = st3[...].astype(BF16)
        w2b[slot, pl.ds(kk * r2, r2), :] = st2[...].astype(BF16)

    @pl.when((e == 0) & (k == 0))
    def _():
        for kk in range(nk):
            cs = copies(0, kk)
            for c in cs:
                c.start()
            for c in cs:
                c.wait()
            land(0, kk)

    @pl.when(has_next)
    def _():
        for c in copies(e + 1, k):
            c.start()

    def compute(slot):
        x = x_ref[...].reshape(bs * R, D)
        u = _dot(x, w1b[slot])
        g = _dot(x, w3b[slot])
        hm = (u * jax.nn.sigmoid(u) * g).astype(BF16)
        gate = gate_ref[...].reshape(bs * R, gate_ref.shape[3])[:, 0:1]
        y_ref[...] = (_dot(hm, w2b[slot]) * gate).astype(BF16).reshape(bs, 1, R, D)

    for slot in range(2):
        @pl.when(e % 2 == slot)
        def _(slot=slot):
            compute(slot)

        @pl.when(has_next & (e % 2 == slot))
        def _(slot=slot):
            for c in copies(e + 1, k):
                c.wait()
            land(1 - slot, k)


def _ffn(xg, gate, w1, w3, w2, layer):
    B, E, R, D = xg.shape
    FF = w1.shape[3]
    bs = _pick(B, (2, 1))
    nk = B // bs
    assert D % nk == 0 and FF % nk == 0
    return pl.pallas_call(
        functools.partial(_ffn_kernel, layer=layer, nk=nk),
        out_shape=jax.ShapeDtypeStruct((B, E, R, D), BF16),
        grid=(E, nk),
        in_specs=[pl.BlockSpec((bs, 1, R, D), lambda e, b: (b, e, 0, 0)),
                  pl.BlockSpec((bs, 1, R, LANES_V7X), lambda e, b: (b, e, 0, 0)),
                  pl.BlockSpec(memory_space=pl.ANY),
                  pl.BlockSpec(memory_space=pl.ANY),
                  pl.BlockSpec(memory_space=pl.ANY)],
        out_specs=pl.BlockSpec((bs, 1, R, D), lambda e, b: (b, e, 0, 0)),
        scratch_shapes=[pltpu.VMEM((2, D, FF), BF16), pltpu.VMEM((2, D, FF), BF16), pltpu.VMEM((2, FF, D), BF16),
                        pltpu.VMEM((D // nk, FF), F32), pltpu.VMEM((D // nk, FF), F32), pltpu.VMEM((FF // nk, D), F32),
                        pltpu.SemaphoreType.DMA((3,))],
        compiler_params=_cparams(("arbitrary", "arbitrary")),
        name="moe_ffn",
    )(xg, gate, w1, w3, w2)


def _combine_kernel(slotc_ref, y_ref, h1_ref, mc_ref, ml_ref, o_ref, *, L, tt, caps):
    E = y_ref.shape[1]
    capc, capl = caps
    i = pl.program_id(2)
    row0 = i * tt
    sc = slotc_ref[0]
    g2 = _mod_rows(mc_ref, ml_ref, row0, tt, L, 5)

    @pl.when(row0 >= L)
    def _():
        jl = lax.broadcasted_iota(jnp.int32, (1, capl), 1).astype(F32)
        hit = jnp.concatenate([jnp.where(sc[:, e:e + 1] == jl, 1.0, 0.0).astype(BF16) for e in range(E)], axis=1)
        yl = y_ref[0, :, 0:capl, :].reshape(E * capl, y_ref.shape[3])
        o_ref[0] = h1_ref[0] + g2 * _dot(hit, yl)

    @pl.when(row0 < L)
    def _():
        jc = lax.broadcasted_iota(jnp.int32, (1, E * capc), 1).astype(F32)
        hit = jnp.zeros((tt, E * capc), F32)
        for e in range(E):
            se = sc[:, e:e + 1]
            hit = hit + jnp.where(jnp.where(se >= 0, se + (e * capc - capl), -1.0) == jc, 1.0, 0.0)
        yc = y_ref[0, :, capl:capl + capc, :].reshape(E * capc, y_ref.shape[3])
        o_ref[0] = h1_ref[0] + g2 * _dot(hit.astype(BF16), yc)


def _combine(slot_c, yff, h1, mc, ml, L, caps):
    B, T, D = h1.shape
    E, R = yff.shape[1], yff.shape[2]
    tt = _pick(L, (256, 128, 64))
    td = _pick(D, (1024, 512, 256, 128))
    return pl.pallas_call(
        functools.partial(_combine_kernel, L=L, tt=tt, caps=caps),
        out_shape=jax.ShapeDtypeStruct((B, T, D), F32),
        grid=(B, D // td, T // tt),
        in_specs=[pl.BlockSpec((1, tt, E), lambda b, j, i: (b, i, 0)),
                  pl.BlockSpec((1, E, R, td), lambda b, j, i: (b, 0, 0, j)),
                  pl.BlockSpec((1, tt, td), lambda b, j, i: (b, i, j)),
                  pl.BlockSpec((6, td), lambda b, j, i: (0, j)),
                  pl.BlockSpec((1, 6, td), lambda b, j, i: (b, 0, j))],
        out_specs=pl.BlockSpec((1, tt, td), lambda b, j, i: (b, i, j)),
        compiler_params=_cparams(("parallel", "parallel", "arbitrary")),
        name="moe_combine",
    )(slot_c, yff, h1, mc, ml)


def _moe(f, logits, h1, mc, ml, w1, w3, w2, layer, L):
    B, T, D = h1.shape
    caps = tuple(max(1, EC_CAPACITY * n // N_EXPERTS) for n in (L, T - L))
    slot, aff = _route(jnp.swapaxes(logits, 1, 2), L, caps)
    xg, gate = _gather(slot, aff, f, L, caps)
    yff = _ffn(xg, gate, w1, w3, w2, layer)
    slot_c = jnp.swapaxes(slot, 1, 2).astype(F32)
    return _combine(slot_c, yff, h1, mc, ml, L, caps)


def _final_kernel(h_ref, g_ref, o_ref):
    h = h_ref[0]
    o_ref[0] = h * lax.rsqrt(jnp.mean(h * h, axis=-1, keepdims=True) + EPS) * g_ref[...]


def _final_norm(h, g, L):
    B, T, D = h.shape
    tr = _pick(L, (256, 128, 64))
    off = L // tr
    return pl.pallas_call(
        _final_kernel,
        out_shape=jax.ShapeDtypeStruct((B, T - L, D), F32),
        grid=(B, (T - L) // tr),
        in_specs=[pl.BlockSpec((1, tr, D), lambda b, i: (b, i + off, 0)),
                  pl.BlockSpec((1, D), lambda b, i: (0, 0))],
        out_specs=pl.BlockSpec((1, tr, D), lambda b, i: (b, i, 0)),
        compiler_params=_cparams(("parallel", "parallel")),
        name="final_norm",
    )(h, g.reshape(1, D))


def kernel(x, c, ctx, c_ctx, ada_w, ada_b, norm_g, final_g, ret_wq, ret_wk, ret_wv, ret_wg, ret_wo, ret_decay, ret_gn_w, ret_gn_b, ml_wq, ml_wk, ml_wv, ml_wog, ml_wgate, ml_bgate, ml_norm_w, ml_wout, na_wqkv, na_rpb, na_wo, hg_wq, hg_wi, hg_wf, hg_wg, hg_norm_w, hg_wo, hg_lb, moe_router, moe_w1, moe_w3, moe_w2):
    B, N, D = x.shape
    L = ctx.shape[1]
    depth = ada_w.shape[0]
    h = jnp.concatenate([ctx, x], axis=1)
    mod_lat, mod_ctx = _mod_vectors(c, c_ctx, ada_w, ada_b)
    lb_all = jnp.cumsum(jax.nn.softmax(hg_lb.astype(F32), axis=0), axis=0)
    zero_b = lambda n: jnp.zeros((1, n), F32)
    gpad = LANES_V7X - 4 * ML_HEADS
    for i in range(depth):
        mc, ml = mod_ctx[i], mod_lat[i]
        kind = i % N_MIXERS
        if kind == 0:
            w = jnp.concatenate([ret_wq, ret_wk, ret_wv, ret_wg], 1).astype(BF16)
            y = _proj(h, norm_g[i, 0], mc, ml, w, zero_b(w.shape[1]), L)
            u = _retention_core(y, ret_decay, ret_gn_w, ret_gn_b, L)
            wo = ret_wo
        elif kind == 1:
            w = jnp.concatenate([ml_wq, ml_wk, ml_wv, ml_wog, ml_wgate[0], ml_wgate[1], jnp.zeros((D, gpad), F32)],
                                1).astype(BF16)
            bias = jnp.concatenate([jnp.zeros((3 * D,), F32), ml_bgate[0], ml_bgate[1], jnp.zeros((gpad,), F32)])[None]
            y = _proj(h, norm_g[i, 0], mc, ml, w, bias, L)
            u = _mlstm_core(y, ml_norm_w, D, L)
            wo = ml_wout
        elif kind == 2:
            w = na_wqkv.astype(BF16)
            y = _proj(h, norm_g[i, 0], mc, ml, w, zero_b(w.shape[1]), L, out_dtype=BF16)
            u = _na_core(y, na_rpb, L)
            wo = na_wo
        else:
            w = jnp.concatenate([hg_wq, hg_wi, hg_wf[0], hg_wf[1], hg_wg], 1).astype(BF16)
            y = _proj(h, norm_g[i, 0], mc, ml, w, zero_b(w.shape[1]), L)
            u = _hgrn2_core(y, lb_all[i] - lb_all[0], hg_norm_w, L)
            wo = hg_wo
        h1, f, logits = _outproj(u, wo.astype(BF16), h, norm_g[i, 1], mc, ml, moe_router[i], L)
        h = _moe(f, logits, h1, mc, ml, moe_w1, moe_w3, moe_w2, i, L)
    return _final_norm(h, final_g, L)
```

```python
import functools

import numpy as np
import jax
import jax.numpy as jnp
from jax import lax
from jax.experimental import pallas as pl
from jax.experimental.pallas import tpu as pltpu

GRID_W = 64
EPS = 1e-6
NEG = -1e30
CHUNK = 64
ROPE_THETA = 10000.0
RET_HEADS = 8
ML_HEADS = 8
NA_HEADS = 16
NA_WIN_R = 8
NA_WIN_C = 16
HG_HEADS = 16
N_EXPERTS = 16
EC_CAPACITY = 2
N_MIXERS = 4

VMEM_LIMIT_BYTES_V7X = 58 * 1024 * 1024
LANES_V7X = 128

F32 = jnp.float32
BF16 = jnp.bfloat16


def _cparams(sem):
    return pltpu.CompilerParams(dimension_semantics=sem, vmem_limit_bytes=VMEM_LIMIT_BYTES_V7X)


def _pick(n, prefs):
    for p in prefs:
        if n % p == 0:
            return p
    return n


def _dot(a, b):
    return jnp.dot(a, b, preferred_element_type=F32)


def _dot_nt(a, b):
    return lax.dot_general(a, b, (((1,), (1,)), ((), ())), preferred_element_type=F32)


def _dot_tn(a, b):
    return lax.dot_general(a, b, (((0,), (0,)), ((), ())), preferred_element_type=F32)


def _norm_mod(h, g, sh, sc):
    ms = jnp.mean(h * h, axis=-1, keepdims=True)
    y = h * lax.rsqrt(ms + EPS) * g
    return y * (1.0 + sc) + sh


def _mod_rows(mc_ref, ml_ref, row0, tm, L, k):
    row = row0 + lax.broadcasted_iota(jnp.int32, (tm, 1), 0)
    return jnp.where(row < L, mc_ref[k:k + 1, :], ml_ref[0, k:k + 1, :])


def _seq_block(T, L):
    cb = _pick(L, (256, 128, 64))
    assert L % cb == 0 and (T - L) % cb == 0
    return cb


def _bwd_block(s, nctx, ntot):
    return nctx - 1 - s if s < nctx else ntot - 1 - (s - nctx)


def _rows(i, n):
    return slice(i * n, (i + 1) * n)


def _mod_kernel(c_ref, w_ref, b_ref, o_ref):
    c = c_ref[...]
    s = (c * jax.nn.sigmoid(c)).astype(BF16)
    o_ref[0] = _dot(s, w_ref[0].astype(BF16)) + b_ref[0]


def _mod_vectors(c, c_ctx, ada_w, ada_b):
    B, D = c.shape
    depth, _, n6 = ada_w.shape
    mp = -(-(B + 1) // 8) * 8
    c_all = jnp.concatenate([c, c_ctx[None], jnp.zeros((mp - B - 1, D), F32)], 0)
    tn = _pick(n6, (1024, 512, 256, 128))
    out = pl.pallas_call(
        _mod_kernel,
        out_shape=jax.ShapeDtypeStruct((depth, mp, n6), F32),
        grid=(depth, n6 // tn),
        in_specs=[pl.BlockSpec((mp, D), lambda i, j: (0, 0)),
                  pl.BlockSpec((1, D, tn), lambda i, j: (i, 0, j)),
                  pl.BlockSpec((1, 1, tn), lambda i, j: (i, 0, j))],
        out_specs=pl.BlockSpec((1, mp, tn), lambda i, j: (i, 0, j)),
        compiler_params=_cparams(("parallel", "parallel")),
        name="mod_vectors",
    )(c_all, ada_w, ada_b.reshape(depth, 1, n6))
    mod_lat = out[:, :B].reshape(depth, B, 6, D)
    mod_ctx = out[:, B].reshape(depth, 6, D)
    return mod_lat, mod_ctx


def _proj_kernel(h_ref, g_ref, mc_ref, ml_ref, w_ref, b_ref, o_ref, a_scr, *, L, tm, ts):
    i = pl.program_id(1)
    j = pl.program_id(2)

    @pl.when(j == 0)
    def _():
        for sub in range(tm // ts):
            rs = slice(sub * ts, (sub + 1) * ts)
            sh = _mod_rows(mc_ref, ml_ref, i * tm + sub * ts, ts, L, 0)
            sc = _mod_rows(mc_ref, ml_ref, i * tm + sub * ts, ts, L, 1)
            a = _norm_mod(h_ref[0, rs, :], g_ref[...], sh, sc).astype(BF16)
            a_scr[rs, :] = a
            o_ref[0, rs, :] = (_dot(a, w_ref[...]) + b_ref[...]).astype(o_ref.dtype)

    @pl.when(j > 0)
    def _():
        o_ref[0] = (_dot(a_scr[...], w_ref[...]) + b_ref[...]).astype(o_ref.dtype)


def _proj(h, g, mc, ml, w, bias, L, out_dtype=F32):
    B, T, D = h.shape
    n = w.shape[1]
    tm = _pick(T, (1152, 768, 576, 384, 256, 128, 64))
    ts = _pick(tm, (384, 128, 64))
    tn = _pick(n, (1024, 896, 512, 256, 128))
    return pl.pallas_call(
        functools.partial(_proj_kernel, L=L, tm=tm, ts=ts),
        out_shape=jax.ShapeDtypeStruct((B, T, n), out_dtype),
        grid=(B, T // tm, n // tn),
        in_specs=[pl.BlockSpec((1, tm, D), lambda b, i, j: (b, i, 0)),
                  pl.BlockSpec((1, D), lambda b, i, j: (0, 0)),
                  pl.BlockSpec((6, D), lambda b, i, j: (0, 0)),
                  pl.BlockSpec((1, 6, D), lambda b, i, j: (b, 0, 0)),
                  pl.BlockSpec((D, tn), lambda b, i, j: (0, j)),
                  pl.BlockSpec((1, tn), lambda b, i, j: (0, j))],
        out_specs=pl.BlockSpec((1, tm, tn), lambda b, i, j: (b, i, j)),
        scratch_shapes=[pltpu.VMEM((tm, D), BF16)],
        compiler_params=_cparams(("parallel", "parallel", "arbitrary")),
        name="proj",
    )(h, g.reshape(1, D), mc, ml, w, bias)


def _head_norm(o, w, b, center):
    if center:
        o = o - jnp.mean(o, axis=-1, keepdims=True)
    o = o * lax.rsqrt(jnp.mean(o * o, axis=-1, keepdims=True) + EPS)
    o = o * w
    if b is not None:
        o = o + b
    return o


def _ret_kernel(dec_ref, q_ref, k_ref, v_ref, g_ref, cos_ref, sin_ref, gw_ref, gb_ref, u_ref,
                s_scr, dm_scr, of_scr, ob_scr, *, dk, C, nctx, ntot, scale):
    h = pl.program_id(1)
    half = dk // 2
    ri = lax.broadcasted_iota(jnp.int32, (C, C), 0)
    ci = lax.broadcasted_iota(jnp.int32, (C, C), 1)
    dif = (ri - ci).astype(F32)
    pos = lax.broadcasted_iota(jnp.int32, (C, 1), 0).astype(F32)
    lg_f = jax.nn.log_sigmoid(jnp.full((1, 1), dec_ref[0, h], F32))
    lg_b = jax.nn.log_sigmoid(jnp.full((1, 1), dec_ref[1, h], F32))
    dm_scr[0] = jnp.where(ri >= ci, jnp.exp(lg_f * dif), 0.0)
    dm_scr[1] = jnp.where(ri <= ci, jnp.exp(-lg_b * dif), 0.0)
    q_dec = (jnp.exp(lg_f * (pos + 1.0)), jnp.exp(lg_b * (C - pos)))
    k_dec = (jnp.exp(lg_f * (C - 1.0 - pos)), jnp.exp(lg_b * pos))
    s_dec = (jnp.exp(lg_f * C), jnp.exp(lg_b * C))
    s_scr[...] = jnp.zeros(s_scr.shape, F32)

    def step(rows, d):
        cos = cos_ref[rows, :]
        sin = sin_ref[rows, :]
        q = q_ref[0, rows, :]
        k = k_ref[0, rows, :]
        q1, q2 = q[:, :half], q[:, half:]
        k1, k2 = k[:, :half], k[:, half:]
        q = jnp.concatenate([q1 * cos - q2 * sin, q1 * sin + q2 * cos], axis=1)
        k = jnp.concatenate([k1 * cos - k2 * sin, k1 * sin + k2 * cos], axis=1) * scale
        vb = v_ref[0, rows, :].astype(BF16)
        a = _dot_nt(q.astype(BF16), k.astype(BF16)) * dm_scr[d]
        st = s_scr[d]
        o = _dot(a.astype(BF16), vb) + _dot_nt((q * q_dec[d]).astype(BF16), st.astype(BF16))
        s_scr[d] = s_dec[d] * st + _dot_tn(vb, (k * k_dec[d]).astype(BF16))
        return o

    for s in range(ntot):
        rf = _rows(s, C)
        of_scr[rf, :] = step(rf, 0)
        rb = _rows(_bwd_block(s, nctx, ntot), C)
        ob_scr[rb, :] = step(rb, 1)

    for i in range(ntot):
        rows = _rows(i, C)
        o = _head_norm(of_scr[rows, :] + ob_scr[rows, :], gw_ref[...], gb_ref[...], True)
        g = g_ref[0, rows, :]
        u_ref[0, rows, :] = (g * jax.nn.sigmoid(g) * o).astype(BF16)


def _rope_tables(N, L, dk):
    t = np.arange(N)
    row = (t // GRID_W).astype(np.float32)
    col = (t % GRID_W).astype(np.float32)
    quarter = dk // 4
    inv = (ROPE_THETA ** (-np.arange(quarter, dtype=np.float32) / quarter)).astype(np.float32)
    ang = np.concatenate([row[:, None] * inv, col[:, None] * inv], -1).astype(np.float32)
    cos = np.concatenate([np.ones((L, dk // 2), np.float32), np.cos(ang)], 0)
    sin = np.concatenate([np.zeros((L, dk // 2), np.float32), np.sin(ang)], 0)
    return jnp.asarray(cos, F32), jnp.asarray(sin, F32)


def _retention_core(y, decay, gn_w, gn_b, L):
    B, T, n4 = y.shape
    D = n4 // 4
    H = RET_HEADS
    dk = D // H
    dv = dk
    C = _seq_block(T, L)
    cos, sin = _rope_tables(T - L, L, dk)
    blk = lambda seg: pl.BlockSpec((1, T, dk), lambda b, h, seg=seg: (b, 0, seg * H + h))
    return pl.pallas_call(
        functools.partial(_ret_kernel, dk=dk, C=C, nctx=L // C, ntot=T // C, scale=dk ** -0.5),
        out_shape=jax.ShapeDtypeStruct((B, T, D), BF16),
        grid=(B, H),
        in_specs=[pl.BlockSpec(memory_space=pltpu.SMEM),
                  blk(0), blk(1), blk(2), blk(3),
                  pl.BlockSpec((T, dk // 2), lambda b, h: (0, 0)),
                  pl.BlockSpec((T, dk // 2), lambda b, h: (0, 0)),
                  pl.BlockSpec((1, dv), lambda b, h: (0, h)),
                  pl.BlockSpec((1, dv), lambda b, h: (0, h))],
        out_specs=pl.BlockSpec((1, T, dv), lambda b, h: (b, 0, h)),
        scratch_shapes=[pltpu.VMEM((2, dv, dk), F32), pltpu.VMEM((2, C, C), F32),
                        pltpu.VMEM((T, dv), F32), pltpu.VMEM((T, dv), F32)],
        compiler_params=_cparams(("parallel", "parallel")),
        name="retention_core",
    )(decay.astype(F32), y, y, y, y, cos, sin, gn_w.reshape(1, D), gn_b.reshape(1, D))


def _chunk_cumsum(g, C, reverse):
    n = g.shape[0]
    pos = lax.broadcasted_iota(jnp.int32, (n, 1), 0) & (C - 1)
    x = g
    s = 1
    while s < C:
        if reverse:
            x = x + jnp.where(pos < C - s, pltpu.roll(x, n - s, axis=0), 0.0)
        else:
            x = x + jnp.where(pos >= s, pltpu.roll(x, s, axis=0), 0.0)
        s *= 2
    return x


def _hg_kernel(q_ref, i_ref, zf_ref, zb_ref, g_ref, lb_ref, nw_ref, u_ref,
               s_scr, of_scr, ob_scr, *, hp, dk, dv, C, CB, nctx, ntot, scale):
    nb = CB // C
    sh = C.bit_length() - 1
    ri = lax.broadcasted_iota(jnp.int32, (CB, CB), 0)
    ci = lax.broadcasted_iota(jnp.int32, (CB, CB), 1)
    same = (ri >> sh) == (ci >> sh)
    m_lo = same & (ri >= ci)
    m_up = same & (ri <= ci)
    s_scr[...] = jnp.zeros(s_scr.shape, F32)

    def block(rows, hh, z_ref, mask, reverse, o_scr, idx):
        cs = slice(hh * dk, (hh + 1) * dk)
        vs = slice(hh * dv, (hh + 1) * dv)
        lb = lb_ref[:, cs]
        qr = q_ref[0, rows, cs]
        q = qr * jax.nn.sigmoid(qr) * scale
        z = z_ref[0, rows, cs]
        sig = jax.nn.sigmoid(z)
        f = lb + (1.0 - lb) * sig
        k = (1.0 - lb) * (1.0 - sig)
        vb = i_ref[0, rows, vs].astype(BF16)
        b = _chunk_cumsum(jnp.log(f), C, reverse)
        b3 = b.reshape(nb, C, dk)
        e = 0 if reverse else C - 1
        b_end3 = b3[:, e:e + 1, :]
        q_in = (q * jnp.exp(b)).astype(BF16)
        k_in = (k * jnp.exp(-b)).astype(BF16)
        k_end = (k.reshape(nb, C, dk) * jnp.exp(b_end3 - b3)).reshape(CB, dk).astype(BF16)
        a = jnp.where(mask, _dot_nt(q_in, k_in), 0.0)
        o = _dot(a.astype(BF16), vb)
        st = s_scr[idx]
        inter = [None] * nb
        for c in (range(nb - 1, -1, -1) if reverse else range(nb)):
            rc = slice(c * C, (c + 1) * C)
            inter[c] = _dot_nt(q_in[rc], st.astype(BF16))
            st = jnp.exp(b_end3[c]) * st + _dot_tn(vb[rc], k_end[rc])
        s_scr[idx] = st
        o_scr[rows, vs] = o + jnp.concatenate(inter, axis=0)

    for s in range(ntot):
        rf = _rows(s, CB)
        rb = _rows(_bwd_block(s, nctx, ntot), CB)
        for hh in range(hp):
            block(rf, hh, zf_ref, m_lo, False, of_scr, 2 * hh)
            block(rb, hh, zb_ref, m_up, True, ob_scr, 2 * hh + 1)

    for i in range(ntot):
        rows = _rows(i, CB)
        for hh in range(hp):
            cs = slice(hh * dv, (hh + 1) * dv)
            o = _head_norm(of_scr[rows, cs] + ob_scr[rows, cs], nw_ref[:, cs], None, False)
            g = g_ref[0, rows, cs]
            u_ref[0, rows, cs] = (g * jax.nn.sigmoid(g) * o).astype(BF16)


def _hgrn2_core(y, lb, norm_w, L):
    B, T, n5 = y.shape
    D = n5 // 5
    H = HG_HEADS
    dk = D // H
    dv = dk
    hp = 2 if H % 2 == 0 else 1
    G = H // hp
    CB = _seq_block(T, L)
    C = min(CHUNK, CB)
    blk = lambda seg: pl.BlockSpec((1, T, hp * dk), lambda b, h, seg=seg: (b, 0, seg * G + h))
    return pl.pallas_call(
        functools.partial(_hg_kernel, hp=hp, dk=dk, dv=dv, C=C, CB=CB, nctx=L // CB, ntot=T // CB, scale=dk ** -0.5),
        out_shape=jax.ShapeDtypeStruct((B, T, D), BF16),
        grid=(B, G),
        in_specs=[blk(0), blk(1), blk(2), blk(3), blk(4),
                  pl.BlockSpec((1, hp * dk), lambda b, h: (0, h)),
                  pl.BlockSpec((1, hp * dv), lambda b, h: (0, h))],
        out_specs=pl.BlockSpec((1, T, hp * dv), lambda b, h: (b, 0, h)),
        scratch_shapes=[pltpu.VMEM((2 * hp, dv, dk), F32), pltpu.VMEM((T, hp * dv), F32),
                        pltpu.VMEM((T, hp * dv), F32)],
        compiler_params=_cparams(("parallel", "parallel")),
        name="hgrn2_core",
    )(y, y, y, y, y, lb.reshape(1, D).astype(F32), norm_w.reshape(1, D))


def _tri(C):
    r = lax.broadcasted_iota(jnp.int32, (C, C), 0)
    c = lax.broadcasted_iota(jnp.int32, (C, C), 1)
    return r >= c, r <= c


def _ml_kernel(q_ref, k_ref, v_ref, og_ref, gt_ref, nw_ref, u_ref, c_scr, n_scr, m_scr, of_scr, ob_scr,
               *, H, C, nctx, ntot, scale):
    h = pl.program_id(1)
    lower, upper = _tri(C)
    c_scr[...] = jnp.zeros(c_scr.shape, F32)
    n_scr[...] = jnp.zeros(n_scr.shape, F32)
    m_scr[...] = jnp.zeros(m_scr.shape, F32)
    gw = gt_ref.shape[2]
    lane = lax.broadcasted_iota(jnp.int32, (C, gw), 1)

    def chunk(rows, d, mask, end_row, o_scr):
        gt = gt_ref[0, rows, :]
        cum = _chunk_cumsum(jax.nn.log_sigmoid(gt), C, d == 1)
        b_col = jnp.sum(jnp.where(lane == h + H + 2 * H * d, cum, 0.0), axis=1, keepdims=True)
        i_col = jnp.sum(jnp.where(lane == h + 2 * H * d, gt, 0.0), axis=1, keepdims=True)
        z = jnp.where(lane == 0, b_col, jnp.where(lane == 1, i_col, 0.0))
        r = jnp.transpose(z)
        b_row, i_row = r[0:1, :], r[1:2, :]
        q = q_ref[0, rows, :]
        k = k_ref[0, rows, :] * scale
        v = v_ref[0, rows, :]
        m_prev = m_scr[d, 0:1, 0:1]
        logw = jnp.where(mask, b_col - b_row + i_row, -jnp.inf)
        log_carry = b_col + m_prev
        m_i = jnp.maximum(log_carry, jnp.max(logw, axis=1, keepdims=True))
        qb, vb = q.astype(BF16), v.astype(BF16)
        s = _dot_nt(qb, k.astype(BF16)) * jnp.exp(logw - m_i)
        a = jnp.exp(log_carry - m_i)
        cst = c_scr[d]
        nst = n_scr[d]
        num = _dot(s.astype(BF16), vb) + a * _dot(qb, cst.astype(BF16))
        den = jnp.sum(s, axis=1, keepdims=True) + a * jnp.sum(q * nst, axis=1, keepdims=True)
        o_scr[rows, :] = num / jnp.maximum(jnp.abs(den), jnp.exp(-m_i))
        b_end = b_col[end_row:end_row + 1, :]
        logw_end = b_end - b_col + i_col
        m_new = jnp.maximum(b_end + m_prev, jnp.max(logw_end, axis=0, keepdims=True))
        decay = jnp.exp(b_end + m_prev - m_new)
        kw = k * jnp.exp(logw_end - m_new)
        c_scr[d] = decay * cst + _dot_tn(kw.astype(BF16), vb)
        n_scr[d] = decay * nst + jnp.sum(kw, axis=0, keepdims=True)
        m_scr[d] = jnp.broadcast_to(m_new, m_scr.shape[1:])

    for s in range(ntot):
        chunk(_rows(s, C), 0, lower, C - 1, of_scr)
        chunk(_rows(_bwd_block(s, nctx, ntot), C), 1, upper, 0, ob_scr)

    for i in range(ntot):
        rows = _rows(i, C)
        o = _head_norm(of_scr[rows, :] + ob_scr[rows, :], nw_ref[...], None, True)
        u_ref[0, rows, :] = (jax.nn.sigmoid(og_ref[0, rows, :]) * o).astype(BF16)


def _mlstm_core(y, norm_w, D, L):
    B, T, n = y.shape
    H = ML_HEADS
    dv = D // H
    dk = dv // 2
    gw = n - 3 * D
    C = _seq_block(T, L)
    return pl.pallas_call(
        functools.partial(_ml_kernel, H=H, C=C, nctx=L // C, ntot=T // C, scale=dk ** -0.5),
        out_shape=jax.ShapeDtypeStruct((B, T, D), BF16),
        grid=(B, H),
        in_specs=[pl.BlockSpec((1, T, dk), lambda b, h: (b, 0, h)),
                  pl.BlockSpec((1, T, dk), lambda b, h: (b, 0, H + h)),
                  pl.BlockSpec((1, T, dv), lambda b, h: (b, 0, H + h)),
                  pl.BlockSpec((1, T, dv), lambda b, h: (b, 0, 2 * H + h)),
                  pl.BlockSpec((1, T, gw), lambda b, h: (b, 0, 3 * D // gw)),
                  pl.BlockSpec((1, dv), lambda b, h: (0, h))],
        out_specs=pl.BlockSpec((1, T, dv), lambda b, h: (b, 0, h)),
        scratch_shapes=[pltpu.VMEM((2, dk, dv), F32), pltpu.VMEM((2, 1, dk), F32), pltpu.VMEM((2, 8, LANES_V7X), F32),
                        pltpu.VMEM((T, dv), F32), pltpu.VMEM((T, dv), F32)],
        compiler_params=_cparams(("parallel", "parallel")),
        name="mlstm_core",
    )(y, y, y, y, y, norm_w.reshape(1, D))


def _na_kernel(q_ref, k_ref, v_ref, toe_ref, o_ref, bias_scr, *, L, rows, wr, G, kr, layouts, cls, scale):
    W = GRID_W

    @pl.when(pl.program_id(1) == 0)
    def _():
        for l, lay in enumerate(layouts):
            for rr, row in enumerate(lay):
                for kk, d in enumerate(row):
                    blk = toe_ref[0, d] if d >= 0 else jnp.full((W, W), NEG, F32)
                    bias_scr[l, rr * W:(rr + 1) * W, kk * W:(kk + 1) * W] = blk

    kc = k_ref[0, 0:L, :].astype(BF16)
    vc = v_ref[0, 0:L, :].astype(BF16)
    qc = q_ref[0, 0:L, :].astype(BF16)
    s = _dot_nt(qc, kc) * scale
    p = jnp.exp(s - jnp.max(s, axis=1, keepdims=True))
    o_ref[0, 0:L, :] = (_dot(p.astype(BF16), vc) / jnp.sum(p, axis=1, keepdims=True)).astype(BF16)

    for g in range(rows // G):
        base = min(max(g * G - wr // 2, 0), rows - kr)
        qrows = slice(L + g * G * W, L + (g + 1) * G * W)
        krows = slice(L + base * W, L + (base + kr) * W)
        q = q_ref[0, qrows, :].astype(BF16)
        kw = k_ref[0, krows, :].astype(BF16)
        vw = v_ref[0, krows, :].astype(BF16)
        s_win = _dot_nt(q, kw) * scale + bias_scr[cls[g]]
        s_ctx = _dot_nt(q, kc) * scale
        m = jnp.maximum(jnp.max(s_win, axis=1, keepdims=True), jnp.max(s_ctx, axis=1, keepdims=True))
        p_win = jnp.exp(s_win - m)
        p_ctx = jnp.exp(s_ctx - m)
        den = jnp.sum(p_win, axis=1, keepdims=True) + jnp.sum(p_ctx, axis=1, keepdims=True)
        o = _dot(p_win.astype(BF16), vw) + _dot(p_ctx.astype(BF16), vc)
        o_ref[0, qrows, :] = (o / den).astype(BF16)


def _na_groups(rows):
    wr = min(NA_WIN_R, rows)
    G = 4 if rows % 4 == 0 else 1
    kr = min(rows, wr + G - 1)
    layouts, cls = [], []
    for g in range(rows // G):
        base = min(max(g * G - wr // 2, 0), rows - kr)
        lay = []
        for rr in range(G):
            r = g * G + rr
            r0 = min(max(r - wr // 2, 0), rows - wr)
            lay.append(tuple((base + kk - r + NA_WIN_R - 1) if r0 <= base + kk < r0 + wr else -1 for kk in range(kr)))
        lay = tuple(lay)
        if lay not in layouts:
            layouts.append(lay)
        cls.append(layouts.index(lay))
    return G, kr, wr, tuple(layouts), tuple(cls)


def _na_rel_blocks(rpb):
    H = rpb.shape[0]
    W = GRID_W
    pad = W - NA_WIN_C
    rp = rpb.astype(F32)
    wide = jnp.concatenate([jnp.repeat(rp[..., :1], pad, axis=-1), rp, jnp.repeat(rp[..., -1:], pad, axis=-1)], -1)
    qi, kcol = np.arange(W)[:, None], np.arange(W)[None, :]
    onehot = (np.arange(2 * W - 1)[:, None, None] == (kcol - qi + W - 1)[None]).astype(np.float32)
    toe = jnp.dot(wide.reshape(-1, 2 * W - 1), jnp.asarray(onehot.reshape(2 * W - 1, W * W)),
                  precision=lax.Precision.HIGHEST).reshape(H, -1, W, W)
    c0 = np.clip(qi - NA_WIN_C // 2, 0, W - NA_WIN_C)
    ok = (kcol >= c0) & (kcol < c0 + NA_WIN_C)
    return jnp.where(jnp.asarray(ok), toe, NEG)


def _na_core(y, rpb, L):
    B, T, n3 = y.shape
    D = n3 // 3
    H = NA_HEADS
    dh = D // H
    rows = (T - L) // GRID_W
    G, kr, wr, layouts, cls = _na_groups(rows)
    toe = _na_rel_blocks(rpb)
    blk = lambda seg: pl.BlockSpec((1, T, dh), lambda h, b, seg=seg: (b, 0, seg * H + h))
    return pl.pallas_call(
        functools.partial(_na_kernel, L=L, rows=rows, wr=wr, G=G, kr=kr, layouts=layouts, cls=cls, scale=dh ** -0.5),
        out_shape=jax.ShapeDtypeStruct((B, T, D), BF16),
        grid=(H, B),
        in_specs=[blk(0), blk(1), blk(2),
                  pl.BlockSpec((1,) + toe.shape[1:], lambda h, b: (h, 0, 0, 0))],
        out_specs=pl.BlockSpec((1, T, dh), lambda h, b: (b, 0, h)),
        scratch_shapes=[pltpu.VMEM((len(layouts), G * GRID_W, kr * GRID_W), F32)],
        compiler_params=_cparams(("arbitrary", "arbitrary")),
        name="na_core",
    )(y, y, y, toe)


def _outproj_kernel(u_ref, wo_ref, h_ref, g_ref, mc_ref, ml_ref, wr_ref, h1_ref, f_ref, lg_ref, o_scr, flo_scr,
                    *, L, tm, nsub, te):
    E = lg_ref.shape[2]
    ts = tm // nsub
    for sub in range(nsub):
        rs = slice(sub * ts, (sub + 1) * ts)
        o_scr[rs, :] = _dot(u_ref[0, rs, :], wo_ref[...])
    for r in range(0, tm, te):
        rs = slice(r, r + te)
        row0 = pl.program_id(1) * tm + r
        g1 = _mod_rows(mc_ref, ml_ref, row0, te, L, 2)
        sh2 = _mod_rows(mc_ref, ml_ref, row0, te, L, 3)
        sc2 = _mod_rows(mc_ref, ml_ref, row0, te, L, 4)
        h1 = h_ref[0, rs, :] + g1 * o_scr[rs, :]
        h1_ref[0, rs, :] = h1
        f = _norm_mod(h1, g_ref[...], sh2, sc2)
        f_hi = f.astype(BF16)
        f_ref[0, rs, :] = f_hi
        flo_scr[rs, :] = (f - f_hi.astype(F32)).astype(BF16)
    r1 = _dot(f_ref[0], wr_ref[...])
    r2 = _dot(flo_scr[...], wr_ref[...])
    lg_ref[0] = r1[:, :E] + r1[:, E:] + r2[:, :E]


def _outproj(u, wo, h, g, mc, ml, w_router, L):
    B, T, D = h.shape
    E = w_router.shape[1]
    wr_hi = w_router.astype(BF16)
    wr_lo = (w_router - wr_hi.astype(F32)).astype(BF16)
    wr2 = jnp.concatenate([wr_hi, wr_lo], axis=1)
    tm = _pick(T, (384, 256, 128, 64))
    nsub = 2 if tm % 32 == 0 else 1
    te = _pick(tm, (64, 32, 16))
    return pl.pallas_call(
        functools.partial(_outproj_kernel, L=L, tm=tm, nsub=nsub, te=te),
        out_shape=(jax.ShapeDtypeStruct((B, T, D), F32), jax.ShapeDtypeStruct((B, T, D), BF16),
                   jax.ShapeDtypeStruct((B, T, E), F32)),
        grid=(B, T // tm),
        in_specs=[pl.BlockSpec((1, tm, D), lambda b, i: (b, i, 0)),
                  pl.BlockSpec((D, D), lambda b, i: (0, 0)),
                  pl.BlockSpec((1, tm, D), lambda b, i: (b, i, 0)),
                  pl.BlockSpec((1, D), lambda b, i: (0, 0)),
                  pl.BlockSpec((6, D), lambda b, i: (0, 0)),
                  pl.BlockSpec((1, 6, D), lambda b, i: (b, 0, 0)),
                  pl.BlockSpec((D, 2 * E), lambda b, i: (0, 0))],
        out_specs=(pl.BlockSpec((1, tm, D), lambda b, i: (b, i, 0)),
                   pl.BlockSpec((1, tm, D), lambda b, i: (b, i, 0)),
                   pl.BlockSpec((1, tm, E), lambda b, i: (b, i, 0))),
        scratch_shapes=[pltpu.VMEM((tm, D), F32), pltpu.VMEM((tm, D), BF16)],
        compiler_params=_cparams(("parallel", "parallel")),
        name="outproj",
    )(u, wo, h, g.reshape(1, D), mc, ml, wr2)


def _route_kernel(lg_ref, slot_ref, aff_ref, before_scr, *, L, caps):
    x = lg_ref[...]
    B, E, T = x.shape
    e = jnp.exp(x - jnp.max(x, axis=1, keepdims=True))
    aff = e / jnp.sum(e, axis=1, keepdims=True)
    aff_ref[...] = aff
    nmax = before_scr.shape[0]
    rb = _pick(nmax, (256, 128, 64))

    def fill(i, carry):
        r = i * rb + lax.broadcasted_iota(jnp.int32, (rb, nmax), 0)
        before_scr[pl.ds(pl.multiple_of(i * rb, rb), rb), :] = jnp.where(
            r < lax.broadcasted_iota(jnp.int32, (rb, nmax), 1), 1.0, 0.0).astype(BF16)
        return carry

    lax.fori_loop(0, nmax // rb, fill, 0)
    base = 0
    for lo, hi, cap in ((L, T, caps[1]), (0, L, caps[0])):
        n = hi - lo
        before = before_scr[0:n, 0:n]
        a = aff[:, :, lo:hi].reshape(B * E, n)
        bits = lax.bitcast_convert_type(a, jnp.int32)

        def search(it, thr, bits=bits, cap=cap):
            cand = thr | lax.shift_left(jnp.int32(1), 30 - it)
            cnt = jnp.sum(jnp.where(bits >= cand, 1.0, 0.0), axis=1, keepdims=True)
            return jnp.where(cnt >= cap, cand, thr)

        thr = lax.fori_loop(0, 31, search, jnp.zeros((B * E, 1), jnp.int32))
        gt = bits > thr
        eq = bits == thr
        need = cap - jnp.sum(jnp.where(gt, 1.0, 0.0), axis=1, keepdims=True)
        rank_eq = _dot(jnp.where(eq, 1.0, 0.0).astype(BF16), before)
        sel = gt | (eq & (rank_eq < need))
        pos = _dot(jnp.where(sel, 1.0, 0.0).astype(BF16), before)
        slot = jnp.where(sel, pos.astype(jnp.int32) + base, -1)
        slot_ref[:, :, lo:hi] = slot.reshape(B, E, n)
        base += cap


def _route(logits_t, L, caps):
    B, E, T = logits_t.shape
    return pl.pallas_call(
        functools.partial(_route_kernel, L=L, caps=caps),
        out_shape=(jax.ShapeDtypeStruct((B, E, T), jnp.int32), jax.ShapeDtypeStruct((B, E, T), F32)),
        scratch_shapes=[pltpu.VMEM((max(L, T - L), max(L, T - L)), BF16)],
        compiler_params=pltpu.CompilerParams(vmem_limit_bytes=VMEM_LIMIT_BYTES_V7X),
        name="route",
    )(logits_t)


def _gather_kernel(slot_ref, aff_ref, f_ref, x_ref, gate_ref, *, L, caps):
    T = f_ref.shape[1]
    capc, capl = caps
    for lo, hi, r0, r1 in ((L, T, 0, capl), (0, L, capl, capl + capc)):
        hit = (lax.broadcasted_iota(jnp.int32, (r1 - r0, hi - lo), 0) + r0) == slot_ref[0, 0, :, lo:hi]
        x_ref[0, 0, r0:r1, :] = _dot(jnp.where(hit, 1.0, 0.0).astype(BF16), f_ref[0, lo:hi, :]).astype(BF16)
        gate = jnp.sum(jnp.where(hit, aff_ref[0, 0, :, lo:hi], 0.0), axis=1, keepdims=True)
        gate_ref[0, 0, r0:r1, :] = jnp.broadcast_to(gate, (r1 - r0, gate_ref.shape[3]))


def _gather(slot, aff, f, L, caps):
    B, E, T = slot.shape
    D = f.shape[2]
    R = caps[0] + caps[1]
    return pl.pallas_call(
        functools.partial(_gather_kernel, L=L, caps=caps),
        out_shape=(jax.ShapeDtypeStruct((B, E, R, D), BF16), jax.ShapeDtypeStruct((B, E, R, LANES_V7X), F32)),
        grid=(B, E),
        in_specs=[pl.BlockSpec((1, 1, 1, T), lambda b, e: (b, e, 0, 0)),
                  pl.BlockSpec((1, 1, 1, T), lambda b, e: (b, e, 0, 0)),
                  pl.BlockSpec((1, T, D), lambda b, e: (b, 0, 0))],
        out_specs=(pl.BlockSpec((1, 1, R, D), lambda b, e: (b, e, 0, 0)),
                   pl.BlockSpec((1, 1, R, LANES_V7X), lambda b, e: (b, e, 0, 0))),
        compiler_params=_cparams(("parallel", "arbitrary")),
        name="moe_gather",
    )(slot.reshape(B, E, 1, T), aff.reshape(B, E, 1, T), f)


def _ffn_kernel(x_ref, gate_ref, w1_hbm, w3_hbm, w2_hbm, y_ref, w1b, w3b, w2b, st1, st3, st2, sems, *, layer, nk):
    e = pl.program_id(0)
    k = pl.program_id(1)
    has_next = e + 1 < pl.num_programs(0)
    bs, _, R, D = x_ref.shape
    r13 = w1b.shape[1] // nk
    r2 = w2b.shape[1] // nk

    def copies(ee, kk):
        return (pltpu.make_async_copy(w1_hbm.at[layer, ee, pl.ds(kk * r13, r13), :], st1, sems.at[0]),
                pltpu.make_async_copy(w3_hbm.at[layer, ee, pl.ds(kk * r13, r13), :], st3, sems.at[1]),
                pltpu.make_async_copy(w2_hbm.at[layer, ee, pl.ds(kk * r2, r2), :], st2, sems.at[2]))

    def land(slot, kk):
        w1b[slot, pl.ds(kk * r13, r13), :] = st1[...].astype(BF16)
        w3b[slot, pl.ds(kk * r13, r13), :] = st3[...].astype(BF16)
        w2b[slot, pl.ds(kk * r2, r2), :] = st2[...].astype(BF16)

    @pl.when((e == 0) & (k == 0))
    def _():
        for kk in range(nk):
            cs = copies(0, kk)
            for c in cs:
                c.start()
            for c in cs:
                c.wait()
            land(0, kk)

    @pl.when(has_next)
    def _():
        for c in copies(e + 1, k):
            c.start()

    def compute(slot):
        x = x_ref[...].reshape(bs * R, D)
        u = _dot(x, w1b[slot])
        g = _dot(x, w3b[slot])
        hm = (u * jax.nn.sigmoid(u) * g).astype(BF16)
        gate = gate_ref[...].reshape(bs * R, gate_ref.shape[3])[:, 0:1]
        y_ref[...] = (_dot(hm, w2b[slot]) * gate).astype(BF16).reshape(bs, 1, R, D)

    for slot in range(2):
        @pl.when(e % 2 == slot)
        def _(slot=slot):
            compute(slot)

        @pl.when(has_next & (e % 2 == slot))
        def _(slot=slot):
            for c in copies(e + 1, k):
                c.wait()
            land(1 - slot, k)


def _ffn(xg, gate, w1, w3, w2, layer):
    B, E, R, D = xg.shape
    FF = w1.shape[3]
    bs = _pick(B, (2, 1))
    nk = B // bs
    assert D % nk == 0 and FF % nk == 0
    return pl.pallas_call(
        functools.partial(_ffn_kernel, layer=layer, nk=nk),
        out_shape=jax.ShapeDtypeStruct((B, E, R, D), BF16),
        grid=(E, nk),
        in_specs=[pl.BlockSpec((bs, 1, R, D), lambda e, b: (b, e, 0, 0)),
                  pl.BlockSpec((bs, 1, R, LANES_V7X), lambda e, b: (b, e, 0, 0)),
                  pl.BlockSpec(memory_space=pl.ANY),
                  pl.BlockSpec(memory_space=pl.ANY),
                  pl.BlockSpec(memory_space=pl.ANY)],
        out_specs=pl.BlockSpec((bs, 1, R, D), lambda e, b: (b, e, 0, 0)),
        scratch_shapes=[pltpu.VMEM((2, D, FF), BF16), pltpu.VMEM((2, D, FF), BF16), pltpu.VMEM((2, FF, D), BF16),
                        pltpu.VMEM((D // nk, FF), F32), pltpu.VMEM((D // nk, FF), F32), pltpu.VMEM((FF // nk, D), F32),
                        pltpu.SemaphoreType.DMA((3,))],
        compiler_params=_cparams(("arbitrary", "arbitrary")),
        name="moe_ffn",
    )(xg, gate, w1, w3, w2)


def _combine_kernel(slotc_ref, y_ref, h1_ref, mc_ref, ml_ref, o_ref, *, L, tt, caps):
    E = y_ref.shape[1]
    capc, capl = caps
    i = pl.program_id(2)
    row0 = i * tt
    sc = slotc_ref[0]
    g2 = _mod_rows(mc_ref, ml_ref, row0, tt, L, 5)

    @pl.when(row0 >= L)
    def _():
        jl = lax.broadcasted_iota(jnp.int32, (1, capl), 1).astype(F32)
        hit = jnp.concatenate([jnp.where(sc[:, e:e + 1] == jl, 1.0, 0.0).astype(BF16) for e in range(E)], axis=1)
        yl = y_ref[0, :, 0:capl, :].reshape(E * capl, y_ref.shape[3])
        o_ref[0] = h1_ref[0] + g2 * _dot(hit, yl)

    @pl.when(row0 < L)
    def _():
        jc = lax.broadcasted_iota(jnp.int32, (1, E * capc), 1).astype(F32)
        hit = jnp.zeros((tt, E * capc), F32)
        for e in range(E):
            se = sc[:, e:e + 1]
            hit = hit + jnp.where(jnp.where(se >= 0, se + (e * capc - capl), -1.0) == jc, 1.0, 0.0)
        yc = y_ref[0, :, capl:capl + capc, :].reshape(E * capc, y_ref.shape[3])
        o_ref[0] = h1_ref[0] + g2 * _dot(hit.astype(BF16), yc)


def _combine(slot_c, yff, h1, mc, ml, L, caps):
    B, T, D = h1.shape
    E, R = yff.shape[1], yff.shape[2]
    tt = _pick(L, (256, 128, 64))
    td = _pick(D, (1024, 512, 256, 128))
    return pl.pallas_call(
        functools.partial(_combine_kernel, L=L, tt=tt, caps=caps),
        out_shape=jax.ShapeDtypeStruct((B, T, D), F32),
        grid=(B, D // td, T // tt),
        in_specs=[pl.BlockSpec((1, tt, E), lambda b, j, i: (b, i, 0)),
                  pl.BlockSpec((1, E, R, td), lambda b, j, i: (b, 0, 0, j)),
                  pl.BlockSpec((1, tt, td), lambda b, j, i: (b, i, j)),
                  pl.BlockSpec((6, td), lambda b, j, i: (0, j)),
                  pl.BlockSpec((1, 6, td), lambda b, j, i: (b, 0, j))],
        out_specs=pl.BlockSpec((1, tt, td), lambda b, j, i: (b, i, j)),
        compiler_params=_cparams(("parallel", "parallel", "arbitrary")),
        name="moe_combine",
    )(slot_c, yff, h1, mc, ml)


def _moe(f, logits, h1, mc, ml, w1, w3, w2, layer, L):
    B, T, D = h1.shape
    caps = tuple(max(1, EC_CAPACITY * n // N_EXPERTS) for n in (L, T - L))
    slot, aff = _route(jnp.swapaxes(logits, 1, 2), L, caps)
    xg, gate = _gather(slot, aff, f, L, caps)
    yff = _ffn(xg, gate, w1, w3, w2, layer)
    slot_c = jnp.swapaxes(slot, 1, 2).astype(F32)
    return _combine(slot_c, yff, h1, mc, ml, L, caps)


def _final_kernel(h_ref, g_ref, o_ref):
    h = h_ref[0]
    o_ref[0] = h * lax.rsqrt(jnp.mean(h * h, axis=-1, keepdims=True) + EPS) * g_ref[...]


def _final_norm(h, g, L):
    B, T, D = h.shape
    tr = _pick(L, (256, 128, 64))
    off = L // tr
    return pl.pallas_call(
        _final_kernel,
        out_shape=jax.ShapeDtypeStruct((B, T - L, D), F32),
        grid=(B, (T - L) // tr),
        in_specs=[pl.BlockSpec((1, tr, D), lambda b, i: (b, i + off, 0)),
                  pl.BlockSpec((1, D), lambda b, i: (0, 0))],
        out_specs=pl.BlockSpec((1, tr, D), lambda b, i: (b, i, 0)),
        compiler_params=_cparams(("parallel", "parallel")),
        name="final_norm",
    )(h, g.reshape(1, D))


def kernel(x, c, ctx, c_ctx, ada_w, ada_b, norm_g, final_g, ret_wq, ret_wk, ret_wv, ret_wg, ret_wo, ret_decay, ret_gn_w, ret_gn_b, ml_wq, ml_wk, ml_wv, ml_wog, ml_wgate, ml_bgate, ml_norm_w, ml_wout, na_wqkv, na_rpb, na_wo, hg_wq, hg_wi, hg_wf, hg_wg, hg_norm_w, hg_wo, hg_lb, moe_router, moe_w1, moe_w3, moe_w2):
    B, N, D = x.shape
    L = ctx.shape[1]
    depth = ada_w.shape[0]
    h = jnp.concatenate([ctx, x], axis=1)
    mod_lat, mod_ctx = _mod_vectors(c, c_ctx, ada_w, ada_b)
    lb_all = jnp.cumsum(jax.nn.softmax(hg_lb.astype(F32), axis=0), axis=0)
    zero_b = lambda n: jnp.zeros((1, n), F32)
    gpad = LANES_V7X - 4 * ML_HEADS
    for i in range(depth):
        mc, ml = mod_ctx[i], mod_lat[i]
        kind = i % N_MIXERS
        if kind == 0:
            w = jnp.concatenate([ret_wq, ret_wk, ret_wv, ret_wg], 1).astype(BF16)
            y = _proj(h, norm_g[i, 0], mc, ml, w, zero_b(w.shape[1]), L)
            u = _retention_core(y, ret_decay, ret_gn_w, ret_gn_b, L)
            wo = ret_wo
        elif kind == 1:
            w = jnp.concatenate([ml_wq, ml_wk, ml_wv, ml_wog, ml_wgate[0], ml_wgate[1], jnp.zeros((D, gpad), F32)],
                                1).astype(BF16)
            bias = jnp.concatenate([jnp.zeros((3 * D,), F32), ml_bgate[0], ml_bgate[1], jnp.zeros((gpad,), F32)])[None]
            y = _proj(h, norm_g[i, 0], mc, ml, w, bias, L)
            u = _mlstm_core(y, ml_norm_w, D, L)
            wo = ml_wout
        elif kind == 2:
            w = na_wqkv.astype(BF16)
            y = _proj(h, norm_g[i, 0], mc, ml, w, zero_b(w.shape[1]), L, out_dtype=BF16)
            u = _na_core(y, na_rpb, L)
            wo = na_wo
        else:
            w = jnp.concatenate([hg_wq, hg_wi, hg_wf[0], hg_wf[1], hg_wg], 1).astype(BF16)
            y = _proj(h, norm_g[i, 0], mc, ml, w, zero_b(w.shape[1]), L)
            u = _hgrn2_core(y, lb_all[i] - lb_all[0], hg_norm_w, L)
            wo = hg_wo
        h1, f, logits = _outproj(u, wo.astype(BF16), h, norm_g[i, 1], mc, ml, moe_router[i], L)
        h = _moe(f, logits, h1, mc, ml, moe_w1, moe_w3, moe_w2, i, L)
    return _final_norm(h, final_g, L)
```
